```python
import math
import jax, jax.numpy as jnp
from jax import lax
import numpy as np

D_MODEL = 1024
BATCH = 32
SEQ = 256
DEPTH = 2
DEC_BATCH = 8
DEC_SEQ = 2048
PAST_LEN = 512

GRID_W = 64
Q_BLOCK = 128
HD = 64
A_HEADS = 6
A_KV = 2
A_WIDTH = A_HEADS * HD
B_HEADS = 6
B_DK = 32
B_WIDTH = B_HEADS * 2 * B_DK
C_WIDTH = 256
D_MIX = A_WIDTH + B_WIDTH + C_WIDTH
SPLITS = (A_HEADS * HD, A_KV * HD, A_KV * HD,
          B_HEADS * 2 * B_DK, B_HEADS * 2 * B_DK, B_WIDTH,
          C_WIDTH, C_WIDTH, C_WIDTH)
D_PROJ = 640 + 1152 + 768
ROPE_BASE = 10000.0
EPS = 1e-6
P_HEADS = 8
N_KEYS = 128
N_EXPERTS = N_KEYS * N_KEYS
P_QDIM = 256
P_HALF = P_QDIM // 2
P_TOPK = 16
P_CHUNK = 128

kernel_name = "hymba_diffusion_peer_step"


def rms_norm(x, g):
    xf = x.astype(jnp.float32)
    y = xf * lax.rsqrt(jnp.mean(xf * xf, axis=-1, keepdims=True) + EPS)
    return y.astype(x.dtype) * g


def axial_rope_tables(n_tokens, dim):
    n_rows = n_tokens // GRID_W
    row = jnp.broadcast_to(jnp.arange(n_rows)[:, None], (n_rows, GRID_W)).reshape(-1)
    col = jnp.broadcast_to(jnp.arange(GRID_W)[None, :], (n_rows, GRID_W)).reshape(-1)
    quarter = dim // 4
    inv = ROPE_BASE ** (-jnp.arange(quarter, dtype=jnp.float32) / quarter)
    ang_r = row.astype(jnp.float32)[:, None] * inv
    ang_c = col.astype(jnp.float32)[:, None] * inv
    return (jnp.cos(ang_r), jnp.sin(ang_r), jnp.cos(ang_c), jnp.sin(ang_c))


def _rotate(x, cos, sin):
    x1, x2 = jnp.split(x, 2, axis=-1)
    return jnp.concatenate([x1 * cos - x2 * sin, x2 * cos + x1 * sin], axis=-1)


def apply_axial_rope(x, tables):
    cr, sr, cc, sc = tables
    extra = x.ndim - 3
    shp = lambda t: t.reshape(t.shape[:1] + (1,) * extra + t.shape[1:]).astype(x.dtype)
    xr, xc = jnp.split(x, 2, axis=-1)
    return jnp.concatenate([_rotate(xr, shp(cr), shp(sr)), _rotate(xc, shp(cc), shp(sc))], axis=-1)


def gqa_attention(q, k, v):
    b, s, h, d = q.shape
    kv = k.shape[2]
    g = h // kv
    nb = s // Q_BLOCK
    qb = q.reshape(b, nb, Q_BLOCK, kv, g, d).transpose(1, 0, 2, 3, 4, 5)
    scale = d ** -0.5

    def one_block(qblk):
        sc = jnp.einsum('bqkgd,blkd->bkgql', qblk, k).astype(jnp.float32) * scale
        p = jax.nn.softmax(sc, axis=-1).astype(v.dtype)
        return jnp.einsum('bkgql,blkd->bqkgd', p, v)

    o = lax.map(one_block, qb)
    return o.transpose(1, 0, 2, 3, 4, 5).reshape(b, s, h * d)


def diff_attention(q, k, v, lam):
    b, s, h, _, dk = q.shape
    nb = s // Q_BLOCK
    qb = q.reshape(b, nb, Q_BLOCK, h, 2, dk).transpose(1, 0, 2, 3, 4, 5)
    scale = dk ** -0.5

    def one_block(qblk):
        sc = jnp.einsum('bqhjd,blhjd->bhjql', qblk, k).astype(jnp.float32) * scale
        p = jax.nn.softmax(sc, axis=-1)
        a = (p[:, :, 0] - lam * p[:, :, 1]).astype(v.dtype)
        return jnp.einsum('bhql,blhe->bqhe', a, v)

    o = lax.map(one_block, qb)
    return o.transpose(1, 0, 2, 3, 4).reshape(b, s, h, v.shape[-1])


def short_conv_mixer(x_in, b_gate, c_gate, w, bias):
    u = c_gate * x_in
    s = u.shape[1]
    up = jnp.pad(u, ((0, 0), (1, 1), (0, 0)))
    y = up[:, :s] * w[0] + up[:, 1:s + 1] * w[1] + up[:, 2:] * w[2] + bias
    return b_gate * y


def peer_ffn(h, wq, sub_keys, u_tab, v_tab):
    b, s, d = h.shape
    x = h.reshape(-1, P_CHUNK, d)

    def one_chunk(xc):
        q = (xc @ wq).reshape(P_CHUNK, P_HEADS, 2, P_HALF)
        sc = jnp.einsum('thcd,hcnd->thcn', q, sub_keys).astype(jnp.float32)
        s1, i1 = lax.top_k(sc[:, :, 0], P_TOPK)
        s2, i2 = lax.top_k(sc[:, :, 1], P_TOPK)
        cand = (s1[..., :, None] + s2[..., None, :]).reshape(P_CHUNK, P_HEADS, P_TOPK * P_TOPK)
        cand_idx = (i1[..., :, None] * N_KEYS + i2[..., None, :]).reshape(P_CHUNK, P_HEADS, P_TOPK * P_TOPK)
        top_s, pos = lax.top_k(cand, P_TOPK)
        eidx = jnp.take_along_axis(cand_idx, pos, axis=-1)
        gate = jax.nn.softmax(top_s, axis=-1).astype(xc.dtype)
        u = jnp.take(u_tab, eidx, axis=0)
        act = jax.nn.gelu(jnp.einsum('thkd,td->thk', u, xc))
        vv = jnp.take(v_tab, eidx, axis=0)
        return jnp.einsum('thk,thkd->td', gate * act, vv)

    return lax.map(one_chunk, x).reshape(b, s, d)


def modulation(cond, ada_w, ada_b):
    m = jax.nn.silu(cond) @ ada_w + ada_b
    return jnp.split(m[:, None, :], 6, axis=-1)


def split_projection(h, w_in, qn_a, kn_a, qn_b, kn_b):
    b, s, _ = h.shape
    p = h @ w_in
    offsets = np.cumsum(np.array(SPLITS))[:-1].tolist()
    qa, ka, va, qb, kb, vb, ci, cb, cc = jnp.split(p, offsets, axis=-1)
    qa = rms_norm(qa.reshape(b, s, A_HEADS, HD), qn_a)
    ka = rms_norm(ka.reshape(b, s, A_KV, HD), kn_a)
    va = va.reshape(b, s, A_KV, HD)
    qb = rms_norm(qb.reshape(b, s, B_HEADS, 2, B_DK), qn_b)
    kb = rms_norm(kb.reshape(b, s, B_HEADS, 2, B_DK), kn_b)
    vb = vb.reshape(b, s, B_HEADS, 2 * B_DK)
    return qa, ka, va, qb, kb, vb, ci, cb, cc


def merge_mixers(oa, ob, oc, lam_init, subln_g, out_norm_a, out_norm_c, w_out):
    b, s = oa.shape[:2]
    ob = rms_norm(ob, subln_g) * (1.0 - lam_init)
    y = jnp.concatenate([rms_norm(oa, out_norm_a), ob.reshape(b, s, B_WIDTH),
                         rms_norm(oc, out_norm_c)], axis=-1)
    return y @ w_out


def setup_inputs(seed: int = 0) -> dict:
    key = jax.random.key(seed)
    ks = iter(jax.random.split(key, 40))
    nrm = lambda shape, scale=1.0: jax.random.normal(next(ks), shape, jnp.float32) * scale
    gain = lambda shape: 1.0 + 0.02 * jax.random.normal(next(ks), shape, jnp.float32)
    return {
        "x_prompt": nrm((BATCH, SEQ, D_MODEL)),
        "x_sample": nrm((DEC_BATCH, DEC_SEQ, D_MODEL)),
        "cache_a_k": nrm((DEC_BATCH, DEPTH, PAST_LEN, A_KV, HD)),
        "cache_a_v": nrm((DEC_BATCH, DEPTH, PAST_LEN, A_KV, HD)),
        "cache_b_k": nrm((DEC_BATCH, DEPTH, PAST_LEN, B_HEADS, 2, B_DK)),
        "cache_b_v": nrm((DEC_BATCH, DEPTH, PAST_LEN, B_HEADS, 2 * B_DK)),
        "c": nrm((DEC_BATCH, D_MODEL)),
        "c_ctx": nrm((D_MODEL,)),
        "ada_w": nrm((DEPTH, D_MODEL, 6 * D_MODEL), 0.5 * D_MODEL ** -0.5),
        "ada_b": nrm((DEPTH, 6 * D_MODEL), 0.02),
        "norm1_g": gain((DEPTH, D_MODEL)),
        "norm2_g": gain((DEPTH, D_MODEL)),
        "w_in": nrm((DEPTH, D_MODEL, D_PROJ), D_MODEL ** -0.5),
        "qn_a": gain((DEPTH, HD)),
        "kn_a": gain((DEPTH, HD)),
        "qn_b": gain((DEPTH, B_DK)),
        "kn_b": gain((DEPTH, B_DK)),
        "lam_q1": nrm((DEPTH, B_DK), 0.1),
        "lam_k1": nrm((DEPTH, B_DK), 0.1),
        "lam_q2": nrm((DEPTH, B_DK), 0.1),
        "lam_k2": nrm((DEPTH, B_DK), 0.1),
        "subln_g": gain((DEPTH, 2 * B_DK)),
        "out_norm_a": gain((DEPTH, A_WIDTH)),
        "out_norm_c": gain((DEPTH, C_WIDTH)),
        "conv_w": nrm((DEPTH, 3, C_WIDTH), 3 ** -0.5),
        "conv_b": nrm((DEPTH, C_WIDTH), 0.02),
        "w_out": nrm((DEPTH, D_MIX, D_MODEL), D_MIX ** -0.5),
        "peer_wq": nrm((DEPTH, D_MODEL, P_HEADS * P_QDIM), D_MODEL ** -0.5),
        "peer_keys": nrm((DEPTH, P_HEADS, 2, N_KEYS, P_HALF), P_HALF ** -0.5),
        "peer_u": nrm((DEPTH, N_EXPERTS, D_MODEL), D_MODEL ** -0.5),
        "peer_v": nrm((DEPTH, N_EXPERTS, D_MODEL), 1.0),
    }


def reference(x_prompt, x_sample, cache_a_k, cache_a_v, cache_b_k, cache_b_v, c, c_ctx,
              ada_w, ada_b, norm1_g, norm2_g, w_in, qn_a, kn_a, qn_b, kn_b,
              lam_q1, lam_k1, lam_q2, lam_k2, subln_g, out_norm_a, out_norm_c,
              conv_w, conv_b, w_out, peer_wq, peer_keys, peer_u, peer_v):
    n_lat = x_sample.shape[1]
    rope_a = axial_rope_tables(n_lat, HD)
    rope_b = axial_rope_tables(n_lat, B_DK)
    xp = x_prompt
    xs = x_sample
    new_ak, new_av, new_bk, new_bv = [], [], [], []
    for l in range(DEPTH):
        lam_init = 0.8 - 0.6 * math.exp(-0.3 * l)
        lam = (jnp.exp(jnp.sum((lam_q1[l] * lam_k1[l]).astype(jnp.float32)))
               - jnp.exp(jnp.sum((lam_q2[l] * lam_k2[l]).astype(jnp.float32))) + lam_init)

        sh1, sc1, g1, sh2, sc2, g2 = modulation(c_ctx[None, :], ada_w[l], ada_b[l])
        h = rms_norm(xp, norm1_g[l]) * (1.0 + sc1) + sh1
        qa, ka, va, qb, kb, vb, ci, cb, cc = split_projection(h, w_in[l], qn_a[l], kn_a[l], qn_b[l], kn_b[l])
        new_ak.append(ka)
        new_av.append(va)
        new_bk.append(kb)
        new_bv.append(vb)
        oa = gqa_attention(qa, ka, va)
        ob = diff_attention(qb, kb, vb, lam)
        oc = short_conv_mixer(ci, cb, cc, conv_w[l], conv_b[l])
        xp = xp + g1 * merge_mixers(oa, ob, oc, lam_init, subln_g[l], out_norm_a[l], out_norm_c[l], w_out[l])
        h = rms_norm(xp, norm2_g[l]) * (1.0 + sc2) + sh2
        xp = xp + g2 * peer_ffn(h, peer_wq[l], peer_keys[l], peer_u[l], peer_v[l])

        sh1, sc1, g1, sh2, sc2, g2 = modulation(c, ada_w[l], ada_b[l])
        h = rms_norm(xs, norm1_g[l]) * (1.0 + sc1) + sh1
        qa, ka, va, qb, kb, vb, ci, cb, cc = split_projection(h, w_in[l], qn_a[l], kn_a[l], qn_b[l], kn_b[l])
        qa = apply_axial_rope(qa, rope_a)
        ka = apply_axial_rope(ka, rope_a)
        qb = apply_axial_rope(qb, rope_b)
        kb = apply_axial_rope(kb, rope_b)
        ka_all = jnp.concatenate([ka, cache_a_k[:, l]], axis=1)
        va_all = jnp.concatenate([va, cache_a_v[:, l]], axis=1)
        kb_all = jnp.concatenate([kb, cache_b_k[:, l]], axis=1)
        vb_all = jnp.concatenate([vb, cache_b_v[:, l]], axis=1)
        oa = gqa_attention(qa, ka_all, va_all)
        ob = diff_attention(qb, kb_all, vb_all, lam)
        oc = short_conv_mixer(ci, cb, cc, conv_w[l], conv_b[l])
        xs = xs + g1 * merge_mixers(oa, ob, oc, lam_init, subln_g[l], out_norm_a[l], out_norm_c[l], w_out[l])
        h = rms_norm(xs, norm2_g[l]) * (1.0 + sc2) + sh2
        xs = xs + g2 * peer_ffn(h, peer_wq[l], peer_keys[l], peer_u[l], peer_v[l])

    new_a_k = jnp.stack(new_ak, axis=1)
    new_a_v = jnp.stack(new_av, axis=1)
    new_b_k = jnp.stack(new_bk, axis=1)
    new_b_v = jnp.stack(new_bv, axis=1)
    return (xp, xs, new_a_k, new_a_v, new_b_k, new_b_v)
```

```python
import functools
import math

import numpy as np
import jax
import jax.numpy as jnp
from jax import lax
from jax.experimental import pallas as pl
from jax.experimental.pallas import tpu as pltpu

F32 = jnp.float32
BF16 = jnp.bfloat16

HD = 64
A_HEADS = 6
A_KV = 2
B_HEADS = 6
B_DK = 32
C_WIDTH = 256
A_WIDTH = A_HEADS * HD
B_WIDTH = B_HEADS * 2 * B_DK
GRID_W = 64
ROPE_BASE = 10000.0
EPS = 1e-6
P_HEADS = 8
N_KEYS = 128
P_HALF = 128
P_TOPK = 16

LANES = 128
SUBLANES = 8
VMEM_LIMIT = 48 * 1024 * 1024

_QA = (0, 384)
_KA2 = (384, 640)
_VA2 = (640, 896)
_QB = (896, 1280)
_KB = (1280, 1664)
_VB = (1664, 2048)
_CI = (2048, 2304)
_CB = (2304, 2560)
_CC = (2560, 2816)
NP_COLS = 2816

NEG_INF = float("-inf")


def _params(sem, vmem=VMEM_LIMIT):
    return pltpu.CompilerParams(dimension_semantics=sem, vmem_limit_bytes=vmem)


def _dot(a, b):
    return jnp.dot(a, b, preferred_element_type=F32)


def _dot_nt(a, b):
    return lax.dot_general(a, b, (((1,), (1,)), ((), ())), preferred_element_type=F32)


def _split3(a):
    hi = a.astype(BF16)
    r = a - hi.astype(F32)
    mid = r.astype(BF16)
    lo = (r - mid.astype(F32)).astype(BF16)
    return hi, mid, lo


def _dot_hp(a, b):
    a0, a1, a2 = _split3(a)
    b0, b1, b2 = _split3(b)
    return (_dot(a0, b0) + (_dot(a0, b1) + _dot(a1, b0))
            + (_dot(a0, b2) + _dot(a1, b1) + _dot(a2, b0)))


def _group_mean_sq(v, g_ref):
    v2 = v * v
    hi = v2.astype(BF16)
    lo = (v2 - hi.astype(F32)).astype(BF16)
    g = g_ref[...]
    return _dot(hi, g) + _dot(lo, g)


def _rms(x):
    return x * lax.rsqrt(jnp.mean(x * x, axis=-1, keepdims=True) + EPS)


def _mod_kernel(cond_ref, w_ref, b_ref, o_ref):
    c = cond_ref[...]
    s = c / (1.0 + jnp.exp(-c))
    o_ref[...] = _dot_hp(s, w_ref[...]) + b_ref[...]


def _modulation(cond, ada_w, ada_b):
    depth, d, d6 = ada_w.shape
    n = cond.shape[0]
    tn = 1536
    return pl.pallas_call(
        _mod_kernel,
        grid=(depth, d6 // tn),
        in_specs=[pl.BlockSpec((n, d), lambda l, j: (0, 0)),
                  pl.BlockSpec((None, d, tn), lambda l, j: (l, 0, j)),
                  pl.BlockSpec((None, 1, tn), lambda l, j: (l, 0, j))],
        out_specs=pl.BlockSpec((None, n, tn), lambda l, j: (l, 0, j)),
        out_shape=jax.ShapeDtypeStruct((depth, n, d6), F32),
        compiler_params=_params(("parallel", "parallel")),
        name="modulation",
    )(cond, ada_w, ada_b.reshape(depth, 1, d6))


def _rope_block(blk, c, sm, sp, shift):
    return (blk * c + pltpu.roll(blk, LANES - shift, 1) * sm + pltpu.roll(blk, shift, 1) * sp)


def _pre_mix_kernel(*refs, rope, emit_cache):
    it = iter(refs)
    x_ref, mod_ref, n1g_ref, win_ref = next(it), next(it), next(it), next(it)
    gqa_ref, gka_ref, gqb_ref, gkb_ref = next(it), next(it), next(it), next(it)
    g64_ref, g32_ref = next(it), next(it)
    if rope:
        ca_ref, sma_ref, spa_ref = next(it), next(it), next(it)
        cb_ref, smb_ref, spb_ref = next(it), next(it), next(it)
    qa_o, ka_o, va_o, qb_o, kb_o, vb_o, u_o, cbg_o = (next(it) for _ in range(8))
    if emit_cache:
        cka_o, cva_o, ckb_o, cvb_o = (next(it) for _ in range(4))

    x = x_ref[...]
    sh1 = mod_ref[0:1, :]
    sc1 = mod_ref[1:2, :]
    h = _rms(x) * n1g_ref[...] * (1.0 + sc1) + sh1
    hb = h.astype(BF16)

    def proj(seg):
        return _dot(hb, win_ref[:, seg[0]:seg[1]])

    def qk(seg, g_ref, gain_ref, width):
        p = proj(seg)
        ms = _group_mean_sq(p, g_ref) if width == 384 else _dot_group256(p, g_ref)
        return p * lax.rsqrt(ms + EPS), gain_ref[...]

    def _dot_group256(p, g_ref):
        v2 = p * p
        hi = v2.astype(BF16)
        lo = (v2 - hi.astype(F32)).astype(BF16)
        g = g_ref[0:256, 0:256]
        return _dot(hi, g) + _dot(lo, g)

    def finish(y, gain, out_ref, tabs, shift, cache_ref=None, cache_cols=None):
        nblk = y.shape[1] // LANES
        for m in range(nblk):
            sl = slice(m * LANES, (m + 1) * LANES)
            blk = y[:, sl] * gain[:, sl]
            if cache_ref is not None and m < cache_cols // LANES:
                cache_ref[:, sl] = blk
            if tabs is not None:
                blk = _rope_block(blk, tabs[0][...], tabs[1][...], tabs[2][...], shift)
            out_ref[:, sl] = blk.astype(out_ref.dtype)

    tabs_a = (ca_ref, sma_ref, spa_ref) if rope else None
    tabs_b = (cb_ref, smb_ref, spb_ref) if rope else None

    y, g = qk(_QA, g64_ref, gqa_ref, 384)
    finish(y, g, qa_o, tabs_a, HD // 4)
    y, g = qk(_KA2, g64_ref, gka_ref, 256)
    finish(y, g, ka_o, tabs_a, HD // 4, cka_o if emit_cache else None, 128)
    va = proj(_VA2)
    va_o[...] = va.astype(va_o.dtype)
    if emit_cache:
        cva_o[...] = va[:, 0:128]
    y, g = qk(_QB, g32_ref, gqb_ref, 384)
    finish(y, g, qb_o, tabs_b, B_DK // 4)
    y, g = qk(_KB, g32_ref, gkb_ref, 384)
    finish(y, g, kb_o, tabs_b, B_DK // 4, ckb_o if emit_cache else None, 384)
    vb = proj(_VB)
    vb_o[...] = vb.astype(vb_o.dtype)
    if emit_cache:
        cvb_o[...] = vb
    u_o[...] = proj(_CC) * proj(_CI)
    cbg_o[...] = proj(_CB)


def _pre_mix(x, mod_l, row_of_tile, n1g, win, gains, gmats, rope_tabs, *, tm, tiles_per_seq, emit_cache):
    t, d = x.shape
    nt = t // tm
    rope = rope_tabs is not None
    const = lambda shape: pl.BlockSpec(shape, lambda i: (0,) * len(shape))
    in_specs = [pl.BlockSpec((tm, d), lambda i: (i, 0)),
                pl.BlockSpec((None, 6, d), lambda i: (row_of_tile(i), 0, 0)),
                const((1, d)), const((d, NP_COLS)),
                const((1, 384)), const((1, 256)), const((1, 384)), const((1, 384)),
                const((384, 384)), const((384, 384))]
    args = [x, mod_l, n1g, win, *gains, *gmats]
    if rope:
        in_specs += [pl.BlockSpec((tm, LANES), lambda i: (i % tiles_per_seq, 0))] * 6
        args += list(rope_tabs)
    widths = [(384, BF16), (256, BF16), (256, BF16), (384, BF16), (384, BF16), (384, BF16),
              (256, F32), (256, F32)]
    if emit_cache:
        widths += [(128, F32), (128, F32), (384, F32), (384, F32)]
    out_shape = [jax.ShapeDtypeStruct((t, w), dt) for w, dt in widths]
    out_specs = [pl.BlockSpec((tm, w), lambda i: (i, 0)) for w, _ in widths]
    return pl.pallas_call(
        functools.partial(_pre_mix_kernel, rope=rope, emit_cache=emit_cache),
        grid=(nt,), in_specs=in_specs, out_specs=out_specs, out_shape=out_shape,
        compiler_params=_params(("parallel",)),
        name="pre_mix_rope" if rope else "pre_mix",
    )(*args)


def _softmax_pv(q, k_parts, v_parts):
    ss = [_dot_nt(q, k) for k in k_parts]
    m = jnp.max(ss[0], axis=-1, keepdims=True)
    for s in ss[1:]:
        m = jnp.maximum(m, jnp.max(s, axis=-1, keepdims=True))
    l = None
    o = None
    for s, v in zip(ss, v_parts):
        p = jnp.exp(s - m)
        ls = jnp.sum(p, axis=-1, keepdims=True)
        os_ = _dot(p.astype(BF16), v)
        l = ls if l is None else l + ls
        o = os_ if o is None else o + os_
    return o * (1.0 / l)


def _attn_kernel(*refs, has_cache, lam_init):
    it = iter(refs)
    qa_ref, ka_ref, va_ref, qb_ref, kb_ref, vb_ref = (next(it) for _ in range(6))
    if has_cache:
        cka_ref, cva_ref, ckb_ref, cvb_ref = (next(it) for _ in range(4))
    lq1_ref, lk1_ref, lq2_ref, lk2_ref = (next(it) for _ in range(4))
    oa_ref, ob_ref = next(it), next(it)

    tq = qa_ref.shape[0]
    lane = lax.broadcasted_iota(jnp.int32, (tq, LANES), 1)
    lo_half = lane < 64

    lam = (jnp.exp(jnp.sum(lq1_ref[...] * lk1_ref[...], axis=-1, keepdims=True))
           - jnp.exp(jnp.sum(lq2_ref[...] * lk2_ref[...], axis=-1, keepdims=True)) + lam_init)

    for m in range(A_HEADS // 2):
        qblk = qa_ref[:, m * LANES:(m + 1) * LANES]
        outs = []
        for par in range(2):
            kv = (2 * m + par) // (A_HEADS // A_KV)
            kblk = 0 if kv == par else 1
            ksl = slice(kblk * LANES, (kblk + 1) * LANES)
            qh = jnp.where(lo_half if par == 0 else jnp.logical_not(lo_half), qblk, jnp.zeros_like(qblk))
            ks, vs = [ka_ref[:, ksl]], [va_ref[:, ksl]]
            if has_cache:
                ks.append(cka_ref[:, ksl])
                vs.append(cva_ref[:, ksl])
            outs.append(_softmax_pv(qh, ks, vs))
        oa_ref[:, m * LANES:(m + 1) * LANES] = jnp.where(lo_half, outs[0], outs[1])

    seg_id = lane >> 5
    for m in range(B_HEADS // 2):
        sl = slice(m * LANES, (m + 1) * LANES)
        qblk = qb_ref[:, sl]
        ks, vs = [kb_ref[:, sl]], [vb_ref[:, sl]]
        if has_cache:
            ks.append(ckb_ref[:, sl])
            vs.append(cvb_ref[:, sl])
        outs = []
        for par in range(2):
            maps = []
            for j in range(2):
                qs = jnp.where(seg_id == (2 * par + j), qblk, jnp.zeros_like(qblk))
                maps.append(_softmax_pv(qs, ks, vs))
            outs.append(maps[0] - lam * maps[1])
        ob_ref[:, sl] = jnp.where(lo_half, outs[0], outs[1])


def _attention(qa, ka2, va2, qb, kb, vb, caches, lams, *, layer, lam_init, tq):
    b, s, _ = qa.shape
    has_cache = caches is not None
    qspec = lambda w: pl.BlockSpec((None, tq, w), lambda i, j: (i, j, 0))
    kspec = lambda w: pl.BlockSpec((None, s, w), lambda i, j: (i, 0, 0))
    in_specs = [qspec(384), kspec(256), kspec(256), qspec(384), kspec(384), kspec(384)]
    args = [qa, ka2, va2, qb, kb, vb]
    if has_cache:
        lc = caches[0].shape[2]
        cspec = lambda w: pl.BlockSpec((None, None, lc, w), lambda i, j: (i, layer, 0, 0))
        in_specs += [cspec(256), cspec(256), cspec(384), cspec(384)]
        args += list(caches)
    in_specs += [pl.BlockSpec((1, B_DK), lambda i, j: (0, 0))] * 4
    args += list(lams)
    return pl.pallas_call(
        functools.partial(_attn_kernel, has_cache=has_cache, lam_init=lam_init),
        grid=(b, s // tq), in_specs=in_specs,
        out_specs=[qspec(384), qspec(384)],
        out_shape=[jax.ShapeDtypeStruct((b, s, 384), F32)] * 2,
        compiler_params=_params(("parallel", "parallel")),
        name="attention_cached" if has_cache else "attention",
    )(*args)


def _post_mix_kernel(oa_ref, ob_ref, u_ref, up_ref, un_ref, cbg_ref, x_ref, mod_ref,
                     ga_ref, gb_ref, g64_ref, gc_ref, cw_ref, cbias_ref, wout_ref, o_ref,
                     *, tiles_per_seq):
    i = pl.program_id(0)
    tm = x_ref.shape[0]
    pos = i % tiles_per_seq
    u = u_ref[...]
    row = lax.broadcasted_iota(jnp.int32, u.shape, 0)
    prev_row = jnp.where(pos == 0, 0.0, 1.0) * up_ref[SUBLANES - 1:SUBLANES, :]
    next_row = jnp.where(pos == tiles_per_seq - 1, 0.0, 1.0) * un_ref[0:1, :]
    u_dn = jnp.where(row == 0, prev_row, pltpu.roll(u, 1, 0))
    u_up = jnp.where(row == tm - 1, next_row, pltpu.roll(u, tm - 1, 0))
    y = u_dn * cw_ref[0:1, :] + u * cw_ref[1:2, :] + u_up * cw_ref[2:3, :] + cbias_ref[...]
    oc = cbg_ref[...] * y

    na = _rms(oa_ref[...]) * ga_ref[...]
    ob = ob_ref[...]
    nb = ob * lax.rsqrt(_group_mean_sq(ob, g64_ref) + EPS) * gb_ref[...]
    nc = _rms(oc) * gc_ref[...]
    mix = (_dot(na.astype(BF16), wout_ref[0:384, :]) + _dot(nb.astype(BF16), wout_ref[384:768, :])
           + _dot(nc.astype(BF16), wout_ref[768:1024, :]))
    o_ref[...] = x_ref[...] + mod_ref[2:3, :] * mix


def _post_mix(oa, ob, u, cbg, x, mod_l, row_of_tile, ga, gb, g64, gc, cw, cbias, wout, *, tm, tiles_per_seq):
    t, d = x.shape
    nt = t // tm
    nb8 = t // SUBLANES
    step = tm // SUBLANES
    const = lambda shape: pl.BlockSpec(shape, lambda i: (0,) * len(shape))
    tok = lambda w: pl.BlockSpec((tm, w), lambda i: (i, 0))
    in_specs = [tok(384), tok(384), tok(256),
                pl.BlockSpec((SUBLANES, 256), lambda i: (jnp.maximum(i * step - 1, 0), 0)),
                pl.BlockSpec((SUBLANES, 256), lambda i: (jnp.minimum((i + 1) * step, nb8 - 1), 0)),
                tok(256), tok(d),
                pl.BlockSpec((None, 6, d), lambda i: (row_of_tile(i), 0, 0)),
                const((1, 384)), const((1, 384)), const((384, 384)), const((1, 256)),
                const((3, 256)), const((1, 256)), const((d, d))]
    return pl.pallas_call(
        functools.partial(_post_mix_kernel, tiles_per_seq=tiles_per_seq),
        grid=(nt,), in_specs=in_specs, out_specs=tok(d),
        out_shape=jax.ShapeDtypeStruct((t, d), F32),
        compiler_params=_params(("parallel",)),
        name="post_mix",
    )(oa, ob, u, u, u, cbg, x, mod_l, ga, gb, g64, gc, cw, cbias, wout)


def _gelu(x):
    c = math.sqrt(2.0 / math.pi)
    return 0.5 * x * (1.0 + jnp.tanh(c * (x + 0.044715 * (x * x * x))))


def _top16(s):
    n = s.shape[0]
    rows = lax.broadcasted_iota(jnp.int32, s.shape, 0).astype(F32)
    work = s
    rank = jnp.full(s.shape, float(P_TOPK), F32)
    vals = []
    for a in range(P_TOPK):
        m = jnp.max(work, axis=0, keepdims=True)
        idx = jnp.min(jnp.where(work == m, rows, float(n)), axis=0, keepdims=True)
        hit = rows == idx
        rank = jnp.where(hit, float(a), rank)
        work = jnp.where(hit, NEG_INF, work)
        vals.append(m)
    return vals, rank


def _stack16(rows_list):
    w = rows_list[0].shape[1]
    r = lax.broadcasted_iota(jnp.int32, (P_TOPK, w), 0)
    out = jnp.zeros((P_TOPK, w), F32)
    for a, v in enumerate(rows_list):
        out = jnp.where(r == a, v, out)
    return out


_CAND_ROWS = 80


def _cand_build(v1_rows, v1_arr, v2_arr, v2_row0, op):
    pieces = [op(v1_rows[0], v2_arr), op(v1_rows[1], v2_arr[0:8])]
    for a in range(2, 8):
        pieces.append(op(v1_rows[a], v2_arr[0:8]))
    pieces.append(op(v1_arr[8:16], v2_row0))
    return jnp.concatenate(pieces, axis=0)


def _cand_index(w):
    row = lax.broadcasted_iota(jnp.int32, (_CAND_ROWS, w), 0)
    a = jnp.where(row < 16, 0, jnp.where(row < 72, 1 + ((row - 16) >> 3), 8 + (row - 72)))
    b = jnp.where(row < 16, row, jnp.where(row < 72, (row - 16) & 7, 0))
    valid = (a + 1) * (b + 1) <= P_TOPK
    flat = (a * P_TOPK + b).astype(F32)
    return valid, flat


def _peer_prologue(x_ref, mod_ref, n2g_ref, wqt_ref, keys_ref, ht_scr, qt_scr,
                   rank2_scr, b_scr, l_scr, a_scr, acc_scr):
    tm = x_ref.shape[0]
    ncb = tm // LANES
    x = x_ref[...]
    h = _rms(x) * n2g_ref[...] * (1.0 + mod_ref[4:5, :]) + mod_ref[3:4, :]
    htb = h.T.astype(BF16)
    ht_scr[...] = htb
    qt = _dot(wqt_ref[...], htb)
    for cb in range(ncb):
        qt_scr[cb] = qt[:, cb * LANES:(cb + 1) * LANES]
    acc_scr[...] = jnp.zeros_like(acc_scr)

    valid, flat = _cand_index(LANES)

    def head_body(hd, carry):
        r1 = pl.multiple_of(hd * (2 * P_HALF), 2 * P_HALF)
        for cb in range(ncb):
            q1 = qt_scr[cb, pl.ds(r1, P_HALF), :].astype(BF16)
            q2 = qt_scr[cb, pl.ds(r1 + P_HALF, P_HALF), :].astype(BF16)
            s1 = _dot(keys_ref[2 * hd], q1)
            s2 = _dot(keys_ref[2 * hd + 1], q2)
            v1_rows, rank1 = _top16(s1)
            v2_rows, rank2 = _top16(s2)
            v1 = _stack16(v1_rows)
            v2 = _stack16(v2_rows)
            e1 = jnp.exp(v1 - v1_rows[0])
            e2 = jnp.exp(v2 - v2_rows[0])
            e1_rows = [e1[a:a + 1, :] for a in range(8)]
            cand = _cand_build(v1_rows, v1, v2, v2_rows[0], lambda p, q: p + q)
            gcand = _cand_build(e1_rows, e1, e2, e2[0:1, :], lambda p, q: p * q)
            work = jnp.where(valid, cand, NEG_INF)
            sel = jnp.zeros_like(cand)
            for _ in range(P_TOPK):
                m = jnp.max(work, axis=0, keepdims=True)
                idx = jnp.min(jnp.where(work == m, flat, 1e9), axis=0, keepdims=True)
                hit = flat == idx
                sel = jnp.where(hit, 1.0, sel)
                work = jnp.where(hit, NEG_INF, work)
            z = jnp.sum(sel * gcand, axis=0, keepdims=True)
            counts = [jnp.sum(sel[0:16], axis=0, keepdims=True)]
            for a in range(1, 8):
                counts.append(jnp.sum(sel[8 + 8 * a:16 + 8 * a], axis=0, keepdims=True))
            for a in range(8, 16):
                counts.append(sel[64 + a:65 + a, :])
            ltab = jnp.zeros_like(rank1)
            for a in range(P_TOPK):
                ltab = jnp.where(rank1 == float(a), counts[a], ltab)
            n = hd * ncb + cb
            l_scr[n] = ltab
            a_scr[n] = jnp.where(rank1 < float(P_TOPK), jnp.exp(s1 - v1_rows[0]) * (1.0 / z), 0.0)
            rank2_scr[n] = rank2
            b_scr[n] = jnp.exp(s2 - v2_rows[0])
        return carry

    lax.fori_loop(0, P_HEADS, head_body, 0)


def _peer_kernel(x_ref, mod_ref, n2g_ref, wqt_ref, keys_ref, u_ref, vt_ref, o_ref,
                 ht_scr, qt_scr, rank2_scr, b_scr, l_scr, a_scr, acc_scr, act_scr, g_scr):
    j = pl.program_id(1)
    nj = pl.num_programs(1)
    tm = x_ref.shape[0]
    ncb = tm // LANES
    ec = u_ref.shape[0]
    nr = ec // N_KEYS

    @pl.when(j == 0)
    def _():
        _peer_prologue(x_ref, mod_ref, n2g_ref, wqt_ref, keys_ref, ht_scr, qt_scr,
                       rank2_scr, b_scr, l_scr, a_scr, acc_scr)

    act_scr[...] = _dot(u_ref[...], ht_scr[...])

    def row_body(r, carry):
        i1 = j * nr + r
        row0 = pl.multiple_of(r * N_KEYS, N_KEYS)
        for cb in range(ncb):
            cs = slice(cb * LANES, (cb + 1) * LANES)
            w = jnp.zeros((N_KEYS, LANES), F32)
            for hd in range(P_HEADS):
                n = hd * ncb + cb
                lrow = l_scr[n, pl.ds(i1, 1), :]
                arow = a_scr[n, pl.ds(i1, 1), :]
                w = w + jnp.where(rank2_scr[n] < lrow, b_scr[n] * arow, 0.0)
            act = act_scr[pl.ds(row0, N_KEYS), cs]
            g_scr[pl.ds(row0, N_KEYS), cs] = (w * _gelu(act)).astype(BF16)
        return carry

    lax.fori_loop(0, nr, row_body, 0)
    acc_scr[...] += _dot(vt_ref[...], g_scr[...])

    @pl.when(j == nj - 1)
    def _():
        o_ref[...] = x_ref[...] + mod_ref[5:6, :] * acc_scr[...].T


def _peer(x, mod_l, row_of_tile, n2g, wqt, keys, u_tab, vt_tab, *, tm, ec):
    t, d = x.shape
    nt = t // tm
    ne = u_tab.shape[0]
    ncb = tm // LANES
    nq = wqt.shape[0]
    tabs = lambda: pltpu.VMEM((P_HEADS * ncb, N_KEYS, LANES), F32)
    return pl.pallas_call(
        _peer_kernel,
        grid=(nt, ne // ec),
        in_specs=[pl.BlockSpec((tm, d), lambda i, j: (i, 0)),
                  pl.BlockSpec((None, 6, d), lambda i, j: (row_of_tile(i), 0, 0)),
                  pl.BlockSpec((1, d), lambda i, j: (0, 0)),
                  pl.BlockSpec((nq, d), lambda i, j: (0, 0)),
                  pl.BlockSpec((2 * P_HEADS, N_KEYS, P_HALF), lambda i, j: (0, 0, 0)),
                  pl.BlockSpec((ec, d), lambda i, j: (j, 0)),
                  pl.BlockSpec((d, ec), lambda i, j: (0, j))],
        out_specs=pl.BlockSpec((tm, d), lambda i, j: (i, 0)),
        out_shape=jax.ShapeDtypeStruct((t, d), F32),
        scratch_shapes=[pltpu.VMEM((d, tm), BF16),
                        pltpu.VMEM((ncb, nq, LANES), F32),
                        tabs(), tabs(), tabs(), tabs(),
                        pltpu.VMEM((d, tm), F32),
                        pltpu.VMEM((ec, tm), F32),
                        pltpu.VMEM((ec, tm), BF16)],
        compiler_params=_params(("parallel", "arbitrary")),
        name="peer",
    )(x, mod_l, n2g, wqt, keys, u_tab, vt_tab)


def _rope_tables(n_tokens, dim):
    quarter = dim // 4
    pos = np.arange(n_tokens)
    row = (pos // GRID_W).astype(np.float32)
    col = (pos % GRID_W).astype(np.float32)
    lane = np.arange(LANES)
    dd = lane % dim
    half = dd // (dim // 2)
    e = dd % (dim // 2)
    second = e // quarter
    freq = e % quarter
    inv = jnp.asarray(ROPE_BASE, F32) ** (-jnp.arange(quarter, dtype=F32) / quarter)
    inv_lane = inv[freq]
    p = jnp.where(jnp.asarray(half == 0)[None, :], jnp.asarray(row)[:, None], jnp.asarray(col)[:, None])
    ang = p * inv_lane[None, :]
    cos, sin = jnp.cos(ang), jnp.sin(ang)
    first = jnp.asarray(second == 0)[None, :]
    return cos, jnp.where(first, -sin, 0.0), jnp.where(first, 0.0, sin)


def _block_diag_mean(width, group):
    idx = np.arange(width) // group
    return jnp.asarray((idx[:, None] == idx[None, :]).astype(np.float32) / group, BF16)


def kernel(x_prompt, x_sample, cache_a_k, cache_a_v, cache_b_k, cache_b_v, c, c_ctx, ada_w, ada_b, norm1_g, norm2_g, w_in, qn_a, kn_a, qn_b, kn_b, lam_q1, lam_k1, lam_q2, lam_k2, subln_g, out_norm_a, out_norm_c, conv_w, conv_b, w_out, peer_wq, peer_keys, peer_u, peer_v):
    nb, s_p, d = x_prompt.shape
    db, s_l, _ = x_sample.shape
    depth = w_in.shape[0]
    past = cache_a_k.shape[2]
    tm = 256
    tq = 256
    ec = 2048
    assert s_p % tm == 0 and s_l % tm == 0 and s_l % GRID_W == 0 and d == 1024

    n_cond = 1 + db
    n_rows = -(-n_cond // SUBLANES) * SUBLANES
    cond = jnp.concatenate([c_ctx[None, :], c, jnp.zeros((n_rows - n_cond, d), F32)], axis=0)
    mod = _modulation(cond, ada_w, ada_b).reshape(depth, n_rows, 6, d)

    def split_cols(w):
        offs = np.cumsum([0, 384, 128, 128, 384, 384, 384, 256, 256, 256])
        return [w[..., offs[k]:offs[k + 1]] for k in range(9)]

    def swap_kv(w):
        return jnp.concatenate([w[..., 64:128], w[..., 0:64]], axis=-1)

    wqa, wka, wva, wqb, wkb, wvb, wci, wcb, wcc = split_cols(w_in)
    win = jnp.concatenate([wqa, wka, swap_kv(wka), wva, swap_kv(wva), wqb, wkb, wvb, wci, wcb, wcc],
                          axis=-1).astype(BF16)
    wout = w_out.astype(BF16)
    wqt = jnp.swapaxes(peer_wq, 1, 2).astype(BF16)
    keys = peer_keys.reshape(depth, 2 * P_HEADS, N_KEYS, P_HALF).astype(BF16)
    u_tab = peer_u.astype(BF16)
    vt_tab = jnp.swapaxes(peer_v, 1, 2).astype(BF16)

    g64 = _block_diag_mean(384, HD)
    g32 = _block_diag_mean(384, B_DK)
    rope_a = _rope_tables(s_l, HD)
    rope_b = _rope_tables(s_l, B_DK)

    ck = cache_a_k.reshape(db, depth, past, 128)
    cv = cache_a_v.reshape(db, depth, past, 128)
    caches = (jnp.concatenate([ck, swap_kv(ck)], axis=-1).astype(BF16),
              jnp.concatenate([cv, swap_kv(cv)], axis=-1).astype(BF16),
              cache_b_k.reshape(db, depth, past, 384).astype(BF16),
              cache_b_v.reshape(db, depth, past, 384).astype(BF16))

    xp = x_prompt.reshape(nb * s_p, d)
    xs = x_sample.reshape(db * s_l, d)
    tps_p = s_p // tm
    tps_l = s_l // tm
    row_p = lambda i: 0
    row_l = lambda i: 1 + i // tps_l

    new = [[], [], [], []]
    for l in range(depth):
        lam_init = 0.8 - 0.6 * math.exp(-0.3 * l)
        gains = ((jnp.tile(qn_a[l], A_HEADS) * HD ** -0.5)[None, :],
                 jnp.tile(kn_a[l], 2 * A_KV)[None, :],
                 (jnp.tile(qn_b[l], 2 * B_HEADS) * B_DK ** -0.5)[None, :],
                 jnp.tile(kn_b[l], 2 * B_HEADS)[None, :])
        lams = (lam_q1[l][None, :], lam_k1[l][None, :], lam_q2[l][None, :], lam_k2[l][None, :])
        gb = (jnp.tile(subln_g[l], B_HEADS) * (1.0 - lam_init))[None, :]
        post_args = (out_norm_a[l][None, :], gb, g64, out_norm_c[l][None, :], conv_w[l], conv_b[l][None, :],
                     wout[l])
        n1g = norm1_g[l][None, :]
        n2g = norm2_g[l][None, :]

        qa, ka2, va2, qb, kb, vb, u, cbg, cka, cva, ckb, cvb = _pre_mix(
            xp, mod[l], row_p, n1g, win[l], gains, (g64, g32), None,
            tm=tm, tiles_per_seq=tps_p, emit_cache=True)
        new[0].append(cka.reshape(nb, s_p, A_KV, HD))
        new[1].append(cva.reshape(nb, s_p, A_KV, HD))
        new[2].append(ckb.reshape(nb, s_p, B_HEADS, 2, B_DK))
        new[3].append(cvb.reshape(nb, s_p, B_HEADS, 2 * B_DK))
        r3 = lambda a: a.reshape(nb, s_p, a.shape[-1])
        oa, ob = _attention(r3(qa), r3(ka2), r3(va2), r3(qb), r3(kb), r3(vb), None, lams,
                            layer=l, lam_init=lam_init, tq=min(tq, s_p))
        xp = _post_mix(oa.reshape(nb * s_p, 384), ob.reshape(nb * s_p, 384), u, cbg, xp, mod[l], row_p,
                       *post_args, tm=tm, tiles_per_seq=tps_p)
        xp = _peer(xp, mod[l], row_p, n2g, wqt[l], keys[l], u_tab[l], vt_tab[l], tm=tm, ec=ec)

        qa, ka2, va2, qb, kb, vb, u, cbg = _pre_mix(
            xs, mod[l], row_l, n1g, win[l], gains, (g64, g32), rope_a + rope_b,
            tm=tm, tiles_per_seq=tps_l, emit_cache=False)
        r3 = lambda a: a.reshape(db, s_l, a.shape[-1])
        oa, ob = _attention(r3(qa), r3(ka2), r3(va2), r3(qb), r3(kb), r3(vb), caches, lams,
                            layer=l, lam_init=lam_init, tq=tq)
        xs = _post_mix(oa.reshape(db * s_l, 384), ob.reshape(db * s_l, 384), u, cbg, xs, mod[l], row_l,
                       *post_args, tm=tm, tiles_per_seq=tps_l)
        xs = _peer(xs, mod[l], row_l, n2g, wqt[l], keys[l], u_tab[l], vt_tab[l], tm=tm, ec=ec)

    return (xp.reshape(nb, s_p, d), xs.reshape(db, s_l, d),
            jnp.stack(new[0], axis=1), jnp.stack(new[1], axis=1),
            jnp.stack(new[2], axis=1), jnp.stack(new[3], axis=1))
```

```python
import functools
import math

import numpy as np
import jax
import jax.numpy as jnp
from jax import lax
from jax.experimental import pallas as pl
from jax.experimental.pallas import tpu as pltpu

F32 = jnp.float32
BF16 = jnp.bfloat16

HD = 64
A_HEADS = 6
A_KV = 2
B_HEADS = 6
B_DK = 32
C_WIDTH = 256
A_WIDTH = A_HEADS * HD
B_WIDTH = B_HEADS * 2 * B_DK
GRID_W = 64
ROPE_BASE = 10000.0
EPS = 1e-6
P_HEADS = 8
N_KEYS = 128
P_HALF = 128
P_TOPK = 16

LANES = 128
SUBLANES = 8
VMEM_LIMIT = 48 * 1024 * 1024

_QA = (0, 384)
_KA2 = (384, 640)
_VA2 = (640, 896)
_QB = (896, 1280)
_KB = (1280, 1664)
_VB = (1664, 2048)
_CI = (2048, 2304)
_CB = (2304, 2560)
_CC = (2560, 2816)
NP_COLS = 2816

NEG_INF = float("-inf")


def _params(sem, vmem=VMEM_LIMIT):
    return pltpu.CompilerParams(dimension_semantics=sem, vmem_limit_bytes=vmem)


def _dot(a, b):
    return jnp.dot(a, b, preferred_element_type=F32)


def _dot_nt(a, b):
    return lax.dot_general(a, b, (((1,), (1,)), ((), ())), preferred_element_type=F32)


def _split3(a):
    hi = a.astype(BF16)
    r = a - hi.astype(F32)
    mid = r.astype(BF16)
    lo = (r - mid.astype(F32)).astype(BF16)
    return hi, mid, lo


def _dot_hp(a, b):
    a0, a1, a2 = _split3(a)
    b0, b1, b2 = _split3(b)
    return (_dot(a0, b0) + (_dot(a0, b1) + _dot(a1, b0))
            + (_dot(a0, b2) + _dot(a1, b1) + _dot(a2, b0)))


def _group_mean_sq(v, g_ref):
    v2 = v * v
    hi = v2.astype(BF16)
    lo = (v2 - hi.astype(F32)).astype(BF16)
    g = g_ref[...]
    return _dot(hi, g) + _dot(lo, g)


def _rms(x):
    return x * lax.rsqrt(jnp.mean(x * x, axis=-1, keepdims=True) + EPS)


def _mod_kernel(cond_ref, w_ref, b_ref, o_ref):
    c = cond_ref[...]
    s = c / (1.0 + jnp.exp(-c))
    o_ref[...] = _dot_hp(s, w_ref[...]) + b_ref[...]


def _modulation(cond, ada_w, ada_b):
    depth, d, d6 = ada_w.shape
    n = cond.shape[0]
    tn = 1536
    return pl.pallas_call(
        _mod_kernel,
        grid=(depth, d6 // tn),
        in_specs=[pl.BlockSpec((n, d), lambda l, j: (0, 0)),
                  pl.BlockSpec((None, d, tn), lambda l, j: (l, 0, j)),
                  pl.BlockSpec((None, 1, tn), lambda l, j: (l, 0, j))],
        out_specs=pl.BlockSpec((None, n, tn), lambda l, j: (l, 0, j)),
        out_shape=jax.ShapeDtypeStruct((depth, n, d6), F32),
        compiler_params=_params(("parallel", "parallel")),
        name="modulation",
    )(cond, ada_w, ada_b.reshape(depth, 1, d6))


def _rope_block(blk, c, sm, sp, shift):
    return (blk * c + pltpu.roll(blk, LANES - shift, 1) * sm + pltpu.roll(blk, shift, 1) * sp)


def _pre_mix_kernel(*refs, rope, emit_cache):
    it = iter(refs)
    x_ref, mod_ref, n1g_ref, win_ref = next(it), next(it), next(it), next(it)
    gqa_ref, gka_ref, gqb_ref, gkb_ref = next(it), next(it), next(it), next(it)
    g64_ref, g32_ref = next(it), next(it)
    if rope:
        ca_ref, sma_ref, spa_ref = next(it), next(it), next(it)
        cb_ref, smb_ref, spb_ref = next(it), next(it), next(it)
    qa_o, ka_o, va_o, qb_o, kb_o, vb_o, u_o, cbg_o = (next(it) for _ in range(8))
    if emit_cache:
        cka_o, cva_o, ckb_o, cvb_o = (next(it) for _ in range(4))

    x = x_ref[...]
    sh1 = mod_ref[0:1, :]
    sc1 = mod_ref[1:2, :]
    h = _rms(x) * n1g_ref[...] * (1.0 + sc1) + sh1
    hb = h.astype(BF16)

    def proj(seg):
        return _dot(hb, win_ref[:, seg[0]:seg[1]])

    def qk(seg, g_ref, gain_ref, width):
        p = proj(seg)
        ms = _group_mean_sq(p, g_ref) if width == 384 else _dot_group256(p, g_ref)
        return p * lax.rsqrt(ms + EPS), gain_ref[...]

    def _dot_group256(p, g_ref):
        v2 = p * p
        hi = v2.astype(BF16)
        lo = (v2 - hi.astype(F32)).astype(BF16)
        g = g_ref[0:256, 0:256]
        return _dot(hi, g) + _dot(lo, g)

    def finish(y, gain, out_ref, tabs, shift, cache_ref=None, cache_cols=None):
        nblk = y.shape[1] // LANES
        for m in range(nblk):
            sl = slice(m * LANES, (m + 1) * LANES)
            blk = y[:, sl] * gain[:, sl]
            if cache_ref is not None and m < cache_cols // LANES:
                cache_ref[:, sl] = blk
            if tabs is not None:
                blk = _rope_block(blk, tabs[0][...], tabs[1][...], tabs[2][...], shift)
            out_ref[:, sl] = blk.astype(out_ref.dtype)

    tabs_a = (ca_ref, sma_ref, spa_ref) if rope else None
    tabs_b = (cb_ref, smb_ref, spb_ref) if rope else None

    y, g = qk(_QA, g64_ref, gqa_ref, 384)
    finish(y, g, qa_o, tabs_a, HD // 4)
    y, g = qk(_KA2, g64_ref, gka_ref, 256)
    finish(y, g, ka_o, tabs_a, HD // 4, cka_o if emit_cache else None, 128)
    va = proj(_VA2)
    va_o[...] = va.astype(va_o.dtype)
    if emit_cache:
        cva_o[...] = va[:, 0:128]
    y, g = qk(_QB, g32_ref, gqb_ref, 384)
    finish(y, g, qb_o, tabs_b, B_DK // 4)
    y, g = qk(_KB, g32_ref, gkb_ref, 384)
    finish(y, g, kb_o, tabs_b, B_DK // 4, ckb_o if emit_cache else None, 384)
    vb = proj(_VB)
    vb_o[...] = vb.astype(vb_o.dtype)
    if emit_cache:
        cvb_o[...] = vb
    u_o[...] = proj(_CC) * proj(_CI)
    cbg_o[...] = proj(_CB)


def _pre_mix(x, mod_l, row_of_tile, n1g, win, gains, gmats, rope_tabs, *, tm, tiles_per_seq, emit_cache):
    t, d = x.shape
    nt = t // tm
    rope = rope_tabs is not None
    const = lambda shape: pl.BlockSpec(shape, lambda i: (0,) * len(shape))
    in_specs = [pl.BlockSpec((tm, d), lambda i: (i, 0)),
                pl.BlockSpec((None, 6, d), lambda i: (row_of_tile(i), 0, 0)),
                const((1, d)), const((d, NP_COLS)),
                const((1, 384)), const((1, 256)), const((1, 384)), const((1, 384)),
                const((384, 384)), const((384, 384))]
    args = [x, mod_l, n1g, win, *gains, *gmats]
    if rope:
        in_specs += [pl.BlockSpec((tm, LANES), lambda i: (i % tiles_per_seq, 0))] * 6
        args += list(rope_tabs)
    widths = [(384, BF16), (256, BF16), (256, BF16), (384, BF16), (384, BF16), (384, BF16),
              (256, F32), (256, F32)]
    if emit_cache:
        widths += [(128, F32), (128, F32), (384, F32), (384, F32)]
    out_shape = [jax.ShapeDtypeStruct((t, w), dt) for w, dt in widths]
    out_specs = [pl.BlockSpec((tm, w), lambda i: (i, 0)) for w, _ in widths]
    return pl.pallas_call(
        functools.partial(_pre_mix_kernel, rope=rope, emit_cache=emit_cache),
        grid=(nt,), in_specs=in_specs, out_specs=out_specs, out_shape=out_shape,
        compiler_params=_params(("parallel",)),
        name="pre_mix_rope" if rope else "pre_mix",
    )(*args)


def _softmax_pv(q, k_parts, v_parts):
    ss = [_dot_nt(q, k) for k in k_parts]
    m = jnp.max(ss[0], axis=-1, keepdims=True)
    for s in ss[1:]:
        m = jnp.maximum(m, jnp.max(s, axis=-1, keepdims=True))
    l = None
    o = None
    for s, v in zip(ss, v_parts):
        p = jnp.exp(s - m)
        ls = jnp.sum(p, axis=-1, keepdims=True)
        os_ = _dot(p.astype(BF16), v)
        l = ls if l is None else l + ls
        o = os_ if o is None else o + os_
    return o * (1.0 / l)


def _attn_kernel(*refs, has_cache, lam_init):
    it = iter(refs)
    qa_ref, ka_ref, va_ref, qb_ref, kb_ref, vb_ref = (next(it) for _ in range(6))
    if has_cache:
        cka_ref, cva_ref, ckb_ref, cvb_ref = (next(it) for _ in range(4))
    lq1_ref, lk1_ref, lq2_ref, lk2_ref = (next(it) for _ in range(4))
    oa_ref, ob_ref = next(it), next(it)

    tq = qa_ref.shape[0]
    lane = lax.broadcasted_iota(jnp.int32, (tq, LANES), 1)
    lo_half = lane < 64

    lam = (jnp.exp(jnp.sum(lq1_ref[...] * lk1_ref[...], axis=-1, keepdims=True))
           - jnp.exp(jnp.sum(lq2_ref[...] * lk2_ref[...], axis=-1, keepdims=True)) + lam_init)

    for m in range(A_HEADS // 2):
        qblk = qa_ref[:, m * LANES:(m + 1) * LANES]
        outs = []
        for par in range(2):
            kv = (2 * m + par) // (A_HEADS // A_KV)
            kblk = 0 if kv == par else 1
            ksl = slice(kblk * LANES, (kblk + 1) * LANES)
            qh = jnp.where(lo_half if par == 0 else jnp.logical_not(lo_half), qblk, jnp.zeros_like(qblk))
            ks, vs = [ka_ref[:, ksl]], [va_ref[:, ksl]]
            if has_cache:
                ks.append(cka_ref[:, ksl])
                vs.append(cva_ref[:, ksl])
            outs.append(_softmax_pv(qh, ks, vs))
        oa_ref[:, m * LANES:(m + 1) * LANES] = jnp.where(lo_half, outs[0], outs[1])

    seg_id = lane >> 5
    for m in range(B_HEADS // 2):
        sl = slice(m * LANES, (m + 1) * LANES)
        qblk = qb_ref[:, sl]
        ks, vs = [kb_ref[:, sl]], [vb_ref[:, sl]]
        if has_cache:
            ks.append(ckb_ref[:, sl])
            vs.append(cvb_ref[:, sl])
        outs = []
        for par in range(2):
            maps = []
            for j in range(2):
                qs = jnp.where(seg_id == (2 * par + j), qblk, jnp.zeros_like(qblk))
                maps.append(_softmax_pv(qs, ks, vs))
            outs.append(maps[0] - lam * maps[1])
        ob_ref[:, sl] = jnp.where(lo_half, outs[0], outs[1])


def _attention(qa, ka2, va2, qb, kb, vb, caches, lams, *, layer, lam_init, tq):
    b, s, _ = qa.shape
    has_cache = caches is not None
    qspec = lambda w: pl.BlockSpec((None, tq, w), lambda i, j: (i, j, 0))
    kspec = lambda w: pl.BlockSpec((None, s, w), lambda i, j: (i, 0, 0))
    in_specs = [qspec(384), kspec(256), kspec(256), qspec(384), kspec(384), kspec(384)]
    args = [qa, ka2, va2, qb, kb, vb]
    if has_cache:
        lc = caches[0].shape[2]
        cspec = lambda w: pl.BlockSpec((None, None, lc, w), lambda i, j: (i, layer, 0, 0))
        in_specs += [cspec(256), cspec(256), cspec(384), cspec(384)]
        args += list(caches)
    in_specs += [pl.BlockSpec((1, B_DK), lambda i, j: (0, 0))] * 4
    args += list(lams)
    return pl.pallas_call(
        functools.partial(_attn_kernel, has_cache=has_cache, lam_init=lam_init),
        grid=(b, s // tq), in_specs=in_specs,
        out_specs=[qspec(384), qspec(384)],
        out_shape=[jax.ShapeDtypeStruct((b, s, 384), F32)] * 2,
        compiler_params=_params(("parallel", "parallel")),
        name="attention_cached" if has_cache else "attention",
    )(*args)


def _post_mix_kernel(oa_ref, ob_ref, u_ref, up_ref, un_ref, cbg_ref, x_ref, mod_ref,
                     ga_ref, gb_ref, g64_ref, gc_ref, cw_ref, cbias_ref, wout_ref, o_ref,
                     *, tiles_per_seq):
    i = pl.program_id(0)
    tm = x_ref.shape[0]
    pos = i % tiles_per_seq
    u = u_ref[...]
    row = lax.broadcasted_iota(jnp.int32, u.shape, 0)
    prev_row = jnp.where(pos == 0, 0.0, 1.0) * up_ref[SUBLANES - 1:SUBLANES, :]
    next_row = jnp.where(pos == tiles_per_seq - 1, 0.0, 1.0) * un_ref[0:1, :]
    u_dn = jnp.where(row == 0, prev_row, pltpu.roll(u, 1, 0))
    u_up = jnp.where(row == tm - 1, next_row, pltpu.roll(u, tm - 1, 0))
    y = u_dn * cw_ref[0:1, :] + u * cw_ref[1:2, :] + u_up * cw_ref[2:3, :] + cbias_ref[...]
    oc = cbg_ref[...] * y

    na = _rms(oa_ref[...]) * ga_ref[...]
    ob = ob_ref[...]
    nb = ob * lax.rsqrt(_group_mean_sq(ob, g64_ref) + EPS) * gb_ref[...]
    nc = _rms(oc) * gc_ref[...]
    mix = (_dot(na.astype(BF16), wout_ref[0:384, :]) + _dot(nb.astype(BF16), wout_ref[384:768, :])
           + _dot(nc.astype(BF16), wout_ref[768:1024, :]))
    o_ref[...] = x_ref[...] + mod_ref[2:3, :] * mix


def _post_mix(oa, ob, u, cbg, x, mod_l, row_of_tile, ga, gb, g64, gc, cw, cbias, wout, *, tm, tiles_per_seq):
    t, d = x.shape
    nt = t // tm
    nb8 = t // SUBLANES
    step = tm // SUBLANES
    const = lambda shape: pl.BlockSpec(shape, lambda i: (0,) * len(shape))
    tok = lambda w: pl.BlockSpec((tm, w), lambda i: (i, 0))
    in_specs = [tok(384), tok(384), tok(256),
                pl.BlockSpec((SUBLANES, 256), lambda i: (jnp.maximum(i * step - 1, 0), 0)),
                pl.BlockSpec((SUBLANES, 256), lambda i: (jnp.minimum((i + 1) * step, nb8 - 1), 0)),
                tok(256), tok(d),
                pl.BlockSpec((None, 6, d), lambda i: (row_of_tile(i), 0, 0)),
                const((1, 384)), const((1, 384)), const((384, 384)), const((1, 256)),
                const((3, 256)), const((1, 256)), const((d, d))]
    return pl.pallas_call(
        functools.partial(_post_mix_kernel, tiles_per_seq=tiles_per_seq),
        grid=(nt,), in_specs=in_specs, out_specs=tok(d),
        out_shape=jax.ShapeDtypeStruct((t, d), F32),
        compiler_params=_params(("parallel",)),
        name="post_mix",
    )(oa, ob, u, u, u, cbg, x, mod_l, ga, gb, g64, gc, cw, cbias, wout)


def _gelu(x):
    k1 = -2.0 * math.sqrt(2.0 / math.pi) * math.log2(math.e)
    return x / (1.0 + jnp.exp2(x * (k1 + (k1 * 0.044715) * (x * x))))


BF16_SUBLANES = 16


def _bcast_row_bf16(row):
    packed = jnp.broadcast_to(row, (BF16_SUBLANES, LANES)).astype(BF16)
    return jnp.concatenate([packed] * (N_KEYS // BF16_SUBLANES), axis=0)


def _top16(s):
    n = s.shape[0]
    rows = lax.broadcasted_iota(jnp.int32, s.shape, 0).astype(F32)
    work = s
    rank = jnp.full(s.shape, float(P_TOPK), F32)
    vals = []
    for a in range(P_TOPK):
        m = jnp.max(work, axis=0, keepdims=True)
        idx = jnp.min(jnp.where(work == m, rows, float(n)), axis=0, keepdims=True)
        hit = rows == idx
        rank = jnp.where(hit, float(a), rank)
        work = jnp.where(hit, NEG_INF, work)
        vals.append(m)
    return vals, rank


def _top16_distinct(s, want_rank):
    work = s
    rank = jnp.full(s.shape, float(P_TOPK), F32) if want_rank else None
    vals = []
    for a in range(P_TOPK):
        m = jnp.max(work, axis=0, keepdims=True)
        hit = work == m
        if want_rank:
            rank = jnp.where(hit, float(a), rank)
        work = jnp.where(hit, NEG_INF, work)
        vals.append(m)
    return vals, rank


def _stack16(rows_list):
    w = rows_list[0].shape[1]
    r = lax.broadcasted_iota(jnp.int32, (P_TOPK, w), 0)
    out = jnp.zeros((P_TOPK, w), F32)
    for a, v in enumerate(rows_list):
        out = jnp.where(r == a, v, out)
    return out


_CAND_ROWS = 80


def _cand_build(v1_rows, v1_arr, v2_arr, v2_row0, op):
    pieces = [op(v1_rows[0], v2_arr), op(v1_rows[1], v2_arr[0:8])]
    for a in range(2, 8):
        pieces.append(op(v1_rows[a], v2_arr[0:8]))
    pieces.append(op(v1_arr[8:16], v2_row0))
    return jnp.concatenate(pieces, axis=0)


def _cand_index(w):
    row = lax.broadcasted_iota(jnp.int32, (_CAND_ROWS, w), 0)
    a = jnp.where(row < 16, 0, jnp.where(row < 72, 1 + ((row - 16) >> 3), 8 + (row - 72)))
    b = jnp.where(row < 16, row, jnp.where(row < 72, (row - 16) & 7, 0))
    valid = (a + 1) * (b + 1) <= P_TOPK
    flat = (a * P_TOPK + b).astype(F32)
    return valid, flat


def _peer_head_tables(s1, s2, valid, flat, exact):
    if exact:
        v1_rows, rank1 = _top16(s1)
        v2_rows, rank2 = _top16(s2)
    else:
        v1_rows, _ = _top16_distinct(s1, False)
        v2_rows, rank2 = _top16_distinct(s2, True)
    v1 = _stack16(v1_rows)
    v2 = _stack16(v2_rows)
    e1 = jnp.exp(v1 - v1_rows[0])
    e2 = jnp.exp(v2 - v2_rows[0])
    e1_rows = [e1[a:a + 1, :] for a in range(8)]
    cand = _cand_build(v1_rows, v1, v2, v2_rows[0], lambda p, q: p + q)
    gcand = _cand_build(e1_rows, e1, e2, e2[0:1, :], lambda p, q: p * q)
    work = jnp.where(valid, cand, NEG_INF)
    sel = jnp.zeros_like(cand)
    for _ in range(P_TOPK):
        m = jnp.max(work, axis=0, keepdims=True)
        idx = jnp.min(jnp.where(work == m, flat, 1e9), axis=0, keepdims=True)
        hit = flat == idx
        sel = jnp.where(hit, 1.0, sel)
        work = jnp.where(hit, NEG_INF, work)
    z = jnp.sum(sel * gcand, axis=0, keepdims=True)
    counts = [jnp.sum(sel[0:16], axis=0, keepdims=True)]
    for a in range(1, 8):
        counts.append(jnp.sum(sel[8 + 8 * a:16 + 8 * a], axis=0, keepdims=True))
    for a in range(8, 16):
        counts.append(sel[64 + a:65 + a, :])
    ltab = jnp.zeros_like(s1)
    if exact:
        member = rank1 < float(P_TOPK)
        for a in range(P_TOPK):
            ltab = jnp.where(rank1 == float(a), counts[a], ltab)
        bad = None
    else:
        member = s1 >= v1_rows[P_TOPK - 1]
        for a in range(P_TOPK):
            ltab = jnp.where(s1 == v1_rows[a], counts[a], ltab)
        n1 = jnp.sum(jnp.where(member, 1.0, 0.0), axis=0, keepdims=True)
        n2 = jnp.sum(jnp.where(rank2 < float(P_TOPK), 1.0, 0.0), axis=0, keepdims=True)
        bad = jnp.where(n1 + n2 != 2.0 * P_TOPK, 1.0, 0.0)
    atab = jnp.where(member, jnp.exp(s1 - v1_rows[0]) * (1.0 / z), 0.0)
    btab = jnp.exp(s2 - v2_rows[0])
    return ltab, atab, rank2, btab, bad


def _peer_prologue(x_ref, mod_ref, n2g_ref, wqt_ref, keys_ref, ht_scr, qt_scr,
                   rank2_scr, b_scr, l_scr, a_scr, acc_scr, bad_scr):
    tm = x_ref.shape[0]
    ncb = tm // LANES
    nq = wqt_ref.shape[0]
    x = x_ref[...]
    h = _rms(x) * n2g_ref[...] * (1.0 + mod_ref[4:5, :]) + mod_ref[3:4, :]
    htb = h.T.astype(BF16)
    ht_scr[...] = htb
    for half in range(2):
        rs = slice(half * (nq // 2), (half + 1) * (nq // 2))
        qt = _dot(wqt_ref[rs, :], htb)
        for cb in range(ncb):
            qt_scr[cb, rs, :] = qt[:, cb * LANES:(cb + 1) * LANES]
    acc_scr[...] = jnp.zeros_like(acc_scr)

    valid, flat = _cand_index(LANES)

    def scores(hd, cb):
        r1 = pl.multiple_of(hd * (2 * P_HALF), 2 * P_HALF)
        q1 = qt_scr[cb, pl.ds(r1, P_HALF), :].astype(BF16)
        q2 = qt_scr[cb, pl.ds(r1 + P_HALF, P_HALF), :].astype(BF16)
        return _dot(keys_ref[2 * hd], q1), _dot(keys_ref[2 * hd + 1], q2)

    def store(hd, cb, ltab, atab, rank2, btab):
        n = hd * ncb + cb
        l_scr[n] = ltab
        a_scr[n] = atab
        rank2_scr[n] = rank2.astype(rank2_scr.dtype)
        b_scr[n] = btab.astype(b_scr.dtype)

    def head_body(hd, carry):
        bad = jnp.zeros((1, LANES), F32)
        for cb in range(ncb):
            s1, s2 = scores(hd, cb)
            ltab, atab, rank2, btab, bad_cb = _peer_head_tables(s1, s2, valid, flat, exact=False)
            store(hd, cb, ltab, atab, rank2, btab)
            bad = jnp.maximum(bad, bad_cb)
        bad_scr[hd] = jnp.broadcast_to(bad, (SUBLANES, LANES))
        return carry

    lax.fori_loop(0, P_HEADS, head_body, 0)

    def redo_body(hd, carry):
        @pl.when(jnp.max(bad_scr[hd]) > 0.0)
        def _():
            for cb in range(ncb):
                s1, s2 = scores(hd, cb)
                ltab, atab, rank2, btab, _ = _peer_head_tables(s1, s2, valid, flat, exact=True)
                store(hd, cb, ltab, atab, rank2, btab)
        return carry

    lax.fori_loop(0, P_HEADS, redo_body, 0)


PEER_SUB = 512


def _peer_kernel(x_ref, mod_ref, n2g_ref, wqt_ref, keys_ref, u_ref, vt_ref, o_ref,
                 ht_scr, qt_scr, rank2_scr, b_scr, l_scr, a_scr, acc_scr, bad_scr, act_scr, g_scr):
    j = pl.program_id(1)
    nj = pl.num_programs(1)
    tm = x_ref.shape[0]
    ncb = tm // LANES
    ec = u_ref.shape[0]
    nsub = ec // PEER_SUB
    rows_per_sub = PEER_SUB // N_KEYS

    @pl.when(j == 0)
    def _():
        _peer_prologue(x_ref, mod_ref, n2g_ref, wqt_ref, keys_ref, ht_scr, qt_scr,
                       rank2_scr, b_scr, l_scr, a_scr, acc_scr, bad_scr)

    ht = ht_scr[...]
    for q in range(nsub):
        act_scr[q] = _dot(u_ref[q * PEER_SUB:(q + 1) * PEER_SUB, :], ht)
    for q in range(nsub):
        for r in range(rows_per_sub):
            i1 = j * (ec // N_KEYS) + (q * rows_per_sub + r)
            rs = slice(r * N_KEYS, (r + 1) * N_KEYS)
            for cb in range(ncb):
                cs = slice(cb * LANES, (cb + 1) * LANES)
                w = jnp.zeros((N_KEYS, LANES), BF16)
                for hd in range(P_HEADS):
                    n = hd * ncb + cb
                    lrow = _bcast_row_bf16(l_scr[n, pl.ds(i1, 1), :])
                    arow = _bcast_row_bf16(a_scr[n, pl.ds(i1, 1), :])
                    w = w + jnp.where(rank2_scr[n] < lrow, b_scr[n] * arow, jnp.zeros_like(w))
                g = w.astype(F32) * _gelu(act_scr[q, rs, cs])
                g_scr[q, rs, cs] = g.astype(BF16)
    for q in range(nsub):
        acc_scr[...] += _dot(vt_ref[:, q * PEER_SUB:(q + 1) * PEER_SUB], g_scr[q])

    @pl.when(j == nj - 1)
    def _():
        o_ref[...] = x_ref[...] + mod_ref[5:6, :] * acc_scr[...].T


def _peer(x, mod_l, row_of_tile, n2g, wqt, keys, u_tab, vt_tab, *, tm, ec):
    t, d = x.shape
    nt = t // tm
    ne = u_tab.shape[0]
    ncb = tm // LANES
    nq = wqt.shape[0]
    tabs = lambda dt: pltpu.VMEM((P_HEADS * ncb, N_KEYS, LANES), dt)
    nsub = ec // PEER_SUB
    return pl.pallas_call(
        _peer_kernel,
        grid=(nt, ne // ec),
        in_specs=[pl.BlockSpec((tm, d), lambda i, j: (i, 0)),
                  pl.BlockSpec((None, 6, d), lambda i, j: (row_of_tile(i), 0, 0)),
                  pl.BlockSpec((1, d), lambda i, j: (0, 0)),
                  pl.BlockSpec((nq, d), lambda i, j: (0, 0)),
                  pl.BlockSpec((2 * P_HEADS, N_KEYS, P_HALF), lambda i, j: (0, 0, 0)),
                  pl.BlockSpec((ec, d), lambda i, j: (j, 0)),
                  pl.BlockSpec((d, ec), lambda i, j: (0, j))],
        out_specs=pl.BlockSpec((tm, d), lambda i, j: (i, 0)),
        out_shape=jax.ShapeDtypeStruct((t, d), F32),
        scratch_shapes=[pltpu.VMEM((d, tm), BF16),
                        pltpu.VMEM((ncb, nq, LANES), F32),
                        tabs(BF16), tabs(BF16), tabs(F32), tabs(F32),
                        pltpu.VMEM((d, tm), F32),
                        pltpu.VMEM((P_HEADS, SUBLANES, LANES), F32),
                        pltpu.VMEM((nsub, PEER_SUB, tm), F32),
                        pltpu.VMEM((nsub, PEER_SUB, tm), BF16)],
        compiler_params=_params(("parallel", "arbitrary")),
        name="peer",
    )(x, mod_l, n2g, wqt, keys, u_tab, vt_tab)


def _rope_tables(n_tokens, dim):
    quarter = dim // 4
    pos = np.arange(n_tokens)
    row = (pos // GRID_W).astype(np.float32)
    col = (pos % GRID_W).astype(np.float32)
    lane = np.arange(LANES)
    dd = lane % dim
    half = dd // (dim // 2)
    e = dd % (dim // 2)
    second = e // quarter
    freq = e % quarter
    inv = jnp.asarray(ROPE_BASE, F32) ** (-jnp.arange(quarter, dtype=F32) / quarter)
    inv_lane = inv[freq]
    p = jnp.where(jnp.asarray(half == 0)[None, :], jnp.asarray(row)[:, None], jnp.asarray(col)[:, None])
    ang = p * inv_lane[None, :]
    cos, sin = jnp.cos(ang), jnp.sin(ang)
    first = jnp.asarray(second == 0)[None, :]
    return cos, jnp.where(first, -sin, 0.0), jnp.where(first, 0.0, sin)


def _block_diag_mean(width, group):
    idx = np.arange(width) // group
    return jnp.asarray((idx[:, None] == idx[None, :]).astype(np.float32) / group, BF16)


def kernel(x_prompt, x_sample, cache_a_k, cache_a_v, cache_b_k, cache_b_v, c, c_ctx, ada_w, ada_b, norm1_g, norm2_g, w_in, qn_a, kn_a, qn_b, kn_b, lam_q1, lam_k1, lam_q2, lam_k2, subln_g, out_norm_a, out_norm_c, conv_w, conv_b, w_out, peer_wq, peer_keys, peer_u, peer_v):
    nb, s_p, d = x_prompt.shape
    db, s_l, _ = x_sample.shape
    depth = w_in.shape[0]
    past = cache_a_k.shape[2]
    tm = 256
    tq = 256
    ec = 2048
    assert s_p % tm == 0 and s_l % tm == 0 and s_l % GRID_W == 0 and d == 1024

    n_cond = 1 + db
    n_rows = -(-n_cond // SUBLANES) * SUBLANES
    cond = jnp.concatenate([c_ctx[None, :], c, jnp.zeros((n_rows - n_cond, d), F32)], axis=0)
    mod = _modulation(cond, ada_w, ada_b).reshape(depth, n_rows, 6, d)

    def split_cols(w):
        offs = np.cumsum([0, 384, 128, 128, 384, 384, 384, 256, 256, 256])
        return [w[..., offs[k]:offs[k + 1]] for k in range(9)]

    def swap_kv(w):
        return jnp.concatenate([w[..., 64:128], w[..., 0:64]], axis=-1)

    wqa, wka, wva, wqb, wkb, wvb, wci, wcb, wcc = split_cols(w_in)
    win = jnp.concatenate([wqa, wka, swap_kv(wka), wva, swap_kv(wva), wqb, wkb, wvb, wci, wcb, wcc],
                          axis=-1).astype(BF16)
    wout = w_out.astype(BF16)
    wqt = jnp.swapaxes(peer_wq, 1, 2).astype(BF16)
    keys = peer_keys.reshape(depth, 2 * P_HEADS, N_KEYS, P_HALF).astype(BF16)
    u_tab = peer_u.astype(BF16)
    vt_tab = jnp.swapaxes(peer_v, 1, 2).astype(BF16)

    g64 = _block_diag_mean(384, HD)
    g32 = _block_diag_mean(384, B_DK)
    rope_a = _rope_tables(s_l, HD)
    rope_b = _rope_tables(s_l, B_DK)

    ck = cache_a_k.reshape(db, depth, past, 128)
    cv = cache_a_v.reshape(db, depth, past, 128)
    caches = (jnp.concatenate([ck, swap_kv(ck)], axis=-1).astype(BF16),
              jnp.concatenate([cv, swap_kv(cv)], axis=-1).astype(BF16),
              cache_b_k.reshape(db, depth, past, 384).astype(BF16),
              cache_b_v.reshape(db, depth, past, 384).astype(BF16))

    xp = x_prompt.reshape(nb * s_p, d)
    xs = x_sample.reshape(db * s_l, d)
    tps_p = s_p // tm
    tps_l = s_l // tm
    row_p = lambda i: 0
    row_l = lambda i: 1 + i // tps_l

    new = [[], [], [], []]
    for l in range(depth):
        lam_init = 0.8 - 0.6 * math.exp(-0.3 * l)
        gains = ((jnp.tile(qn_a[l], A_HEADS) * HD ** -0.5)[None, :],
                 jnp.tile(kn_a[l], 2 * A_KV)[None, :],
                 (jnp.tile(qn_b[l], 2 * B_HEADS) * B_DK ** -0.5)[None, :],
                 jnp.tile(kn_b[l], 2 * B_HEADS)[None, :])
        lams = (lam_q1[l][None, :], lam_k1[l][None, :], lam_q2[l][None, :], lam_k2[l][None, :])
        gb = (jnp.tile(subln_g[l], B_HEADS) * (1.0 - lam_init))[None, :]
        post_args = (out_norm_a[l][None, :], gb, g64, out_norm_c[l][None, :], conv_w[l], conv_b[l][None, :],
                     wout[l])
        n1g = norm1_g[l][None, :]
        n2g = norm2_g[l][None, :]

        qa, ka2, va2, qb, kb, vb, u, cbg, cka, cva, ckb, cvb = _pre_mix(
            xp, mod[l], row_p, n1g, win[l], gains, (g64, g32), None,
            tm=tm, tiles_per_seq=tps_p, emit_cache=True)
        new[0].append(cka.reshape(nb, s_p, A_KV, HD))
        new[1].append(cva.reshape(nb, s_p, A_KV, HD))
        new[2].append(ckb.reshape(nb, s_p, B_HEADS, 2, B_DK))
        new[3].append(cvb.reshape(nb, s_p, B_HEADS, 2 * B_DK))
        r3 = lambda a: a.reshape(nb, s_p, a.shape[-1])
        oa, ob = _attention(r3(qa), r3(ka2), r3(va2), r3(qb), r3(kb), r3(vb), None, lams,
                            layer=l, lam_init=lam_init, tq=min(tq, s_p))
        xp = _post_mix(oa.reshape(nb * s_p, 384), ob.reshape(nb * s_p, 384), u, cbg, xp, mod[l], row_p,
                       *post_args, tm=tm, tiles_per_seq=tps_p)
        xp = _peer(xp, mod[l], row_p, n2g, wqt[l], keys[l], u_tab[l], vt_tab[l], tm=tm, ec=ec)

        qa, ka2, va2, qb, kb, vb, u, cbg = _pre_mix(
            xs, mod[l], row_l, n1g, win[l], gains, (g64, g32), rope_a + rope_b,
            tm=tm, tiles_per_seq=tps_l, emit_cache=False)
        r3 = lambda a: a.reshape(db, s_l, a.shape[-1])
        oa, ob = _attention(r3(qa), r3(ka2), r3(va2), r3(qb), r3(kb), r3(vb), caches, lams,
                            layer=l, lam_init=lam_init, tq=tq)
        xs = _post_mix(oa.reshape(db * s_l, 384), ob.reshape(db * s_l, 384), u, cbg, xs, mod[l], row_l,
                       *post_args, tm=tm, tiles_per_seq=tps_l)
        xs = _peer(xs, mod[l], row_l, n2g, wqt[l], keys[l], u_tab[l], vt_tab[l], tm=tm, ec=ec)

    return (xp.reshape(nb, s_p, d), xs.reshape(db, s_l, d),
            jnp.stack(new[0], axis=1), jnp.stack(new[1], axis=1),
            jnp.stack(new[2], axis=1), jnp.stack(new[3], axis=1))
```

```python
import functools
import math

import numpy as np
import jax
import jax.numpy as jnp
from jax import lax
from jax.experimental import pallas as pl
from jax.experimental.pallas import tpu as pltpu

F32 = jnp.float32
BF16 = jnp.bfloat16

HD = 64
A_HEADS = 6
A_KV = 2
B_HEADS = 6
B_DK = 32
C_WIDTH = 256
A_WIDTH = A_HEADS * HD
B_WIDTH = B_HEADS * 2 * B_DK
GRID_W = 64
ROPE_BASE = 10000.0
EPS = 1e-6
P_HEADS = 8
N_KEYS = 128
P_HALF = 128
P_TOPK = 16

LANES = 128
SUBLANES = 8
VMEM_LIMIT = 48 * 1024 * 1024

_QA = (0, 384)
_KA2 = (384, 640)
_VA2 = (640, 896)
_QB = (896, 1280)
_KB = (1280, 1664)
_VB = (1664, 2048)
_CI = (2048, 2304)
_CB = (2304, 2560)
_CC = (2560, 2816)
NP_COLS = 2816

NEG_INF = float("-inf")


def _params(sem, vmem=VMEM_LIMIT):
    return pltpu.CompilerParams(dimension_semantics=sem, vmem_limit_bytes=vmem)


def _dot(a, b):
    return jnp.dot(a, b, preferred_element_type=F32)


def _dot_nt(a, b):
    return lax.dot_general(a, b, (((1,), (1,)), ((), ())), preferred_element_type=F32)


def _split3(a):
    hi = a.astype(BF16)
    r = a - hi.astype(F32)
    mid = r.astype(BF16)
    lo = (r - mid.astype(F32)).astype(BF16)
    return hi, mid, lo


def _dot_hp(a, b):
    a0, a1, a2 = _split3(a)
    b0, b1, b2 = _split3(b)
    return (_dot(a0, b0) + (_dot(a0, b1) + _dot(a1, b0))
            + (_dot(a0, b2) + _dot(a1, b1) + _dot(a2, b0)))


def _group_mean_sq(v, g_ref):
    v2 = v * v
    hi = v2.astype(BF16)
    lo = (v2 - hi.astype(F32)).astype(BF16)
    g = g_ref[...]
    return _dot(hi, g) + _dot(lo, g)


def _rms(x):
    return x * lax.rsqrt(jnp.mean(x * x, axis=-1, keepdims=True) + EPS)


def _mod_kernel(cond_ref, w_ref, b_ref, o_ref):
    c = cond_ref[...]
    s = c / (1.0 + jnp.exp(-c))
    o_ref[...] = _dot_hp(s, w_ref[...]) + b_ref[...]


def _modulation(cond, ada_w, ada_b):
    depth, d, d6 = ada_w.shape
    n = cond.shape[0]
    tn = 1536
    return pl.pallas_call(
        _mod_kernel,
        grid=(depth, d6 // tn),
        in_specs=[pl.BlockSpec((n, d), lambda l, j: (0, 0)),
                  pl.BlockSpec((None, d, tn), lambda l, j: (l, 0, j)),
                  pl.BlockSpec((None, 1, tn), lambda l, j: (l, 0, j))],
        out_specs=pl.BlockSpec((None, n, tn), lambda l, j: (l, 0, j)),
        out_shape=jax.ShapeDtypeStruct((depth, n, d6), F32),
        compiler_params=_params(("parallel", "parallel")),
        name="modulation",
    )(cond, ada_w, ada_b.reshape(depth, 1, d6))


def _rope_block(blk, c, sm, sp, shift):
    return (blk * c + pltpu.roll(blk, LANES - shift, 1) * sm + pltpu.roll(blk, shift, 1) * sp)


def _pre_mix_kernel(*refs, rope, emit_cache):
    it = iter(refs)
    x_ref, mod_ref, n1g_ref, win_ref = next(it), next(it), next(it), next(it)
    gqa_ref, gka_ref, gqb_ref, gkb_ref = next(it), next(it), next(it), next(it)
    g64_ref, g32_ref = next(it), next(it)
    if rope:
        ca_ref, sma_ref, spa_ref = next(it), next(it), next(it)
        cb_ref, smb_ref, spb_ref = next(it), next(it), next(it)
    qa_o, ka_o, va_o, qb_o, kb_o, vb_o, u_o, cbg_o = (next(it) for _ in range(8))
    if emit_cache:
        cka_o, cva_o, ckb_o, cvb_o = (next(it) for _ in range(4))

    x = x_ref[...]
    sh1 = mod_ref[0:1, :]
    sc1 = mod_ref[1:2, :]
    h = _rms(x) * n1g_ref[...] * (1.0 + sc1) + sh1
    hb = h.astype(BF16)

    def proj(seg):
        return _dot(hb, win_ref[:, seg[0]:seg[1]])

    def qk(seg, g_ref, gain_ref, width):
        p = proj(seg)
        ms = _group_mean_sq(p, g_ref) if width == 384 else _dot_group256(p, g_ref)
        return p * lax.rsqrt(ms + EPS), gain_ref[...]

    def _dot_group256(p, g_ref):
        v2 = p * p
        hi = v2.astype(BF16)
        lo = (v2 - hi.astype(F32)).astype(BF16)
        g = g_ref[0:256, 0:256]
        return _dot(hi, g) + _dot(lo, g)

    def finish(y, gain, out_ref, tabs, shift, cache_ref=None, cache_cols=None):
        nblk = y.shape[1] // LANES
        for m in range(nblk):
            sl = slice(m * LANES, (m + 1) * LANES)
            blk = y[:, sl] * gain[:, sl]
            if cache_ref is not None and m < cache_cols // LANES:
                cache_ref[:, sl] = blk
            if tabs is not None:
                blk = _rope_block(blk, tabs[0][...], tabs[1][...], tabs[2][...], shift)
            out_ref[:, sl] = blk.astype(out_ref.dtype)

    tabs_a = (ca_ref, sma_ref, spa_ref) if rope else None
    tabs_b = (cb_ref, smb_ref, spb_ref) if rope else None

    y, g = qk(_QA, g64_ref, gqa_ref, 384)
    finish(y, g, qa_o, tabs_a, HD // 4)
    y, g = qk(_KA2, g64_ref, gka_ref, 256)
    finish(y, g, ka_o, tabs_a, HD // 4, cka_o if emit_cache else None, 128)
    va = proj(_VA2)
    va_o[...] = va.astype(va_o.dtype)
    if emit_cache:
        cva_o[...] = va[:, 0:128]
    y, g = qk(_QB, g32_ref, gqb_ref, 384)
    finish(y, g, qb_o, tabs_b, B_DK // 4)
    y, g = qk(_KB, g32_ref, gkb_ref, 384)
    finish(y, g, kb_o, tabs_b, B_DK // 4, ckb_o if emit_cache else None, 384)
    vb = proj(_VB)
    vb_o[...] = vb.astype(vb_o.dtype)
    if emit_cache:
        cvb_o[...] = vb
    u_o[...] = proj(_CC) * proj(_CI)
    cbg_o[...] = proj(_CB)


def _pre_mix(x, mod_l, row_of_tile, n1g, win, gains, gmats, rope_tabs, *, tm, tiles_per_seq, emit_cache):
    t, d = x.shape
    nt = t // tm
    rope = rope_tabs is not None
    const = lambda shape: pl.BlockSpec(shape, lambda i: (0,) * len(shape))
    in_specs = [pl.BlockSpec((tm, d), lambda i: (i, 0)),
                pl.BlockSpec((None, 6, d), lambda i: (row_of_tile(i), 0, 0)),
                const((1, d)), const((d, NP_COLS)),
                const((1, 384)), const((1, 256)), const((1, 384)), const((1, 384)),
                const((384, 384)), const((384, 384))]
    args = [x, mod_l, n1g, win, *gains, *gmats]
    if rope:
        in_specs += [pl.BlockSpec((tm, LANES), lambda i: (i % tiles_per_seq, 0))] * 6
        args += list(rope_tabs)
    widths = [(384, BF16), (256, BF16), (256, BF16), (384, BF16), (384, BF16), (384, BF16),
              (256, F32), (256, F32)]
    if emit_cache:
        widths += [(128, F32), (128, F32), (384, F32), (384, F32)]
    out_shape = [jax.ShapeDtypeStruct((t, w), dt) for w, dt in widths]
    out_specs = [pl.BlockSpec((tm, w), lambda i: (i, 0)) for w, _ in widths]
    return pl.pallas_call(
        functools.partial(_pre_mix_kernel, rope=rope, emit_cache=emit_cache),
        grid=(nt,), in_specs=in_specs, out_specs=out_specs, out_shape=out_shape,
        compiler_params=_params(("parallel",)),
        name="pre_mix_rope" if rope else "pre_mix",
    )(*args)


def _softmax_pv(q, k_parts, v_parts):
    ss = [_dot_nt(q, k) for k in k_parts]
    m = jnp.max(ss[0], axis=-1, keepdims=True)
    for s in ss[1:]:
        m = jnp.maximum(m, jnp.max(s, axis=-1, keepdims=True))
    l = None
    o = None
    for s, v in zip(ss, v_parts):
        p = jnp.exp(s - m)
        ls = jnp.sum(p, axis=-1, keepdims=True)
        os_ = _dot(p.astype(BF16), v)
        l = ls if l is None else l + ls
        o = os_ if o is None else o + os_
    return o * (1.0 / l)


def _attn_kernel(*refs, has_cache, lam_init):
    it = iter(refs)
    qa_ref, ka_ref, va_ref, qb_ref, kb_ref, vb_ref = (next(it) for _ in range(6))
    if has_cache:
        cka_ref, cva_ref, ckb_ref, cvb_ref = (next(it) for _ in range(4))
    lq1_ref, lk1_ref, lq2_ref, lk2_ref = (next(it) for _ in range(4))
    oa_ref, ob_ref = next(it), next(it)

    tq = qa_ref.shape[0]
    lane = lax.broadcasted_iota(jnp.int32, (tq, LANES), 1)
    lo_half = lane < 64

    lam = (jnp.exp(jnp.sum(lq1_ref[...] * lk1_ref[...], axis=-1, keepdims=True))
           - jnp.exp(jnp.sum(lq2_ref[...] * lk2_ref[...], axis=-1, keepdims=True)) + lam_init)

    for m in range(A_HEADS // 2):
        qblk = qa_ref[:, m * LANES:(m + 1) * LANES]
        outs = []
        for par in range(2):
            kv = (2 * m + par) // (A_HEADS // A_KV)
            kblk = 0 if kv == par else 1
            ksl = slice(kblk * LANES, (kblk + 1) * LANES)
            qh = jnp.where(lo_half if par == 0 else jnp.logical_not(lo_half), qblk, jnp.zeros_like(qblk))
            ks, vs = [ka_ref[:, ksl]], [va_ref[:, ksl]]
            if has_cache:
                ks.append(cka_ref[:, ksl])
                vs.append(cva_ref[:, ksl])
            outs.append(_softmax_pv(qh, ks, vs))
        oa_ref[:, m * LANES:(m + 1) * LANES] = jnp.where(lo_half, outs[0], outs[1])

    seg_id = lane >> 5
    for m in range(B_HEADS // 2):
        sl = slice(m * LANES, (m + 1) * LANES)
        qblk = qb_ref[:, sl]
        ks, vs = [kb_ref[:, sl]], [vb_ref[:, sl]]
        if has_cache:
            ks.append(ckb_ref[:, sl])
            vs.append(cvb_ref[:, sl])
        outs = []
        for par in range(2):
            maps = []
            for j in range(2):
                qs = jnp.where(seg_id == (2 * par + j), qblk, jnp.zeros_like(qblk))
                maps.append(_softmax_pv(qs, ks, vs))
            outs.append(maps[0] - lam * maps[1])
        ob_ref[:, sl] = jnp.where(lo_half, outs[0], outs[1])


def _attention(qa, ka2, va2, qb, kb, vb, caches, lams, *, layer, lam_init, tq):
    b, s, _ = qa.shape
    has_cache = caches is not None
    qspec = lambda w: pl.BlockSpec((None, tq, w), lambda i, j: (i, j, 0))
    kspec = lambda w: pl.BlockSpec((None, s, w), lambda i, j: (i, 0, 0))
    in_specs = [qspec(384), kspec(256), kspec(256), qspec(384), kspec(384), kspec(384)]
    args = [qa, ka2, va2, qb, kb, vb]
    if has_cache:
        lc = caches[0].shape[2]
        cspec = lambda w: pl.BlockSpec((None, None, lc, w), lambda i, j: (i, layer, 0, 0))
        in_specs += [cspec(256), cspec(256), cspec(384), cspec(384)]
        args += list(caches)
    in_specs += [pl.BlockSpec((1, B_DK), lambda i, j: (0, 0))] * 4
    args += list(lams)
    return pl.pallas_call(
        functools.partial(_attn_kernel, has_cache=has_cache, lam_init=lam_init),
        grid=(b, s // tq), in_specs=in_specs,
        out_specs=[qspec(384), qspec(384)],
        out_shape=[jax.ShapeDtypeStruct((b, s, 384), F32)] * 2,
        compiler_params=_params(("parallel", "parallel")),
        name="attention_cached" if has_cache else "attention",
    )(*args)


def _post_mix_kernel(oa_ref, ob_ref, u_ref, up_ref, un_ref, cbg_ref, x_ref, mod_ref,
                     ga_ref, gb_ref, g64_ref, gc_ref, cw_ref, cbias_ref, wout_ref, o_ref,
                     *, tiles_per_seq):
    i = pl.program_id(0)
    tm = x_ref.shape[0]
    pos = i % tiles_per_seq
    u = u_ref[...]
    row = lax.broadcasted_iota(jnp.int32, u.shape, 0)
    prev_row = jnp.where(pos == 0, 0.0, 1.0) * up_ref[SUBLANES - 1:SUBLANES, :]
    next_row = jnp.where(pos == tiles_per_seq - 1, 0.0, 1.0) * un_ref[0:1, :]
    u_dn = jnp.where(row == 0, prev_row, pltpu.roll(u, 1, 0))
    u_up = jnp.where(row == tm - 1, next_row, pltpu.roll(u, tm - 1, 0))
    y = u_dn * cw_ref[0:1, :] + u * cw_ref[1:2, :] + u_up * cw_ref[2:3, :] + cbias_ref[...]
    oc = cbg_ref[...] * y

    na = _rms(oa_ref[...]) * ga_ref[...]
    ob = ob_ref[...]
    nb = ob * lax.rsqrt(_group_mean_sq(ob, g64_ref) + EPS) * gb_ref[...]
    nc = _rms(oc) * gc_ref[...]
    mix = (_dot(na.astype(BF16), wout_ref[0:384, :]) + _dot(nb.astype(BF16), wout_ref[384:768, :])
           + _dot(nc.astype(BF16), wout_ref[768:1024, :]))
    o_ref[...] = x_ref[...] + mod_ref[2:3, :] * mix


def _post_mix(oa, ob, u, cbg, x, mod_l, row_of_tile, ga, gb, g64, gc, cw, cbias, wout, *, tm, tiles_per_seq):
    t, d = x.shape
    nt = t // tm
    nb8 = t // SUBLANES
    step = tm // SUBLANES
    const = lambda shape: pl.BlockSpec(shape, lambda i: (0,) * len(shape))
    tok = lambda w: pl.BlockSpec((tm, w), lambda i: (i, 0))
    in_specs = [tok(384), tok(384), tok(256),
                pl.BlockSpec((SUBLANES, 256), lambda i: (jnp.maximum(i * step - 1, 0), 0)),
                pl.BlockSpec((SUBLANES, 256), lambda i: (jnp.minimum((i + 1) * step, nb8 - 1), 0)),
                tok(256), tok(d),
                pl.BlockSpec((None, 6, d), lambda i: (row_of_tile(i), 0, 0)),
                const((1, 384)), const((1, 384)), const((384, 384)), const((1, 256)),
                const((3, 256)), const((1, 256)), const((d, d))]
    return pl.pallas_call(
        functools.partial(_post_mix_kernel, tiles_per_seq=tiles_per_seq),
        grid=(nt,), in_specs=in_specs, out_specs=tok(d),
        out_shape=jax.ShapeDtypeStruct((t, d), F32),
        compiler_params=_params(("parallel",)),
        name="post_mix",
    )(oa, ob, u, u, u, cbg, x, mod_l, ga, gb, g64, gc, cw, cbias, wout)


def _gelu(x):
    k1 = -2.0 * math.sqrt(2.0 / math.pi) * math.log2(math.e)
    return x / (1.0 + jnp.exp2(x * (k1 + (k1 * 0.044715) * (x * x))))


BF16_SUBLANES = 16


def _bcast_row_bf16(row):
    packed = jnp.broadcast_to(row, (BF16_SUBLANES, LANES)).astype(BF16)
    return jnp.concatenate([packed] * (N_KEYS // BF16_SUBLANES), axis=0)


def _top16(s):
    n = s.shape[0]
    rows = lax.broadcasted_iota(jnp.int32, s.shape, 0).astype(F32)
    work = s
    rank = jnp.full(s.shape, float(P_TOPK), F32)
    vals = []
    for a in range(P_TOPK):
        m = jnp.max(work, axis=0, keepdims=True)
        idx = jnp.min(jnp.where(work == m, rows, float(n)), axis=0, keepdims=True)
        hit = rows == idx
        rank = jnp.where(hit, float(a), rank)
        work = jnp.where(hit, NEG_INF, work)
        vals.append(m)
    return vals, rank


def _top16_distinct(s, want_rank):
    work = s
    rank = jnp.full(s.shape, float(P_TOPK), F32) if want_rank else None
    vals = []
    for a in range(P_TOPK):
        m = jnp.max(work, axis=0, keepdims=True)
        hit = work == m
        if want_rank:
            rank = jnp.where(hit, float(a), rank)
        work = jnp.where(hit, NEG_INF, work)
        vals.append(m)
    return vals, rank


def _stack16(rows_list):
    w = rows_list[0].shape[1]
    r = lax.broadcasted_iota(jnp.int32, (P_TOPK, w), 0)
    out = jnp.zeros((P_TOPK, w), F32)
    for a, v in enumerate(rows_list):
        out = jnp.where(r == a, v, out)
    return out


_CAND_ROWS = 80


def _cand_build(v1_rows, v1_arr, v2_arr, v2_row0, op):
    pieces = [op(v1_rows[0], v2_arr), op(v1_rows[1], v2_arr[0:8])]
    for a in range(2, 8):
        pieces.append(op(v1_rows[a], v2_arr[0:8]))
    pieces.append(op(v1_arr[8:16], v2_row0))
    return jnp.concatenate(pieces, axis=0)


def _cand_index(w):
    row = lax.broadcasted_iota(jnp.int32, (_CAND_ROWS, w), 0)
    a = jnp.where(row < 16, 0, jnp.where(row < 72, 1 + ((row - 16) >> 3), 8 + (row - 72)))
    b = jnp.where(row < 16, row, jnp.where(row < 72, (row - 16) & 7, 0))
    valid = (a + 1) * (b + 1) <= P_TOPK
    flat = (a * P_TOPK + b).astype(F32)
    return valid, flat


def _peer_head_tables(s1, s2, valid, flat, exact):
    if exact:
        v1_rows, rank1 = _top16(s1)
        v2_rows, rank2 = _top16(s2)
    else:
        v1_rows, _ = _top16_distinct(s1, False)
        v2_rows, rank2 = _top16_distinct(s2, True)
    v1 = _stack16(v1_rows)
    v2 = _stack16(v2_rows)
    e1 = jnp.exp(v1 - v1_rows[0])
    e2 = jnp.exp(v2 - v2_rows[0])
    e1_rows = [e1[a:a + 1, :] for a in range(8)]
    cand = _cand_build(v1_rows, v1, v2, v2_rows[0], lambda p, q: p + q)
    gcand = _cand_build(e1_rows, e1, e2, e2[0:1, :], lambda p, q: p * q)
    work = jnp.where(valid, cand, NEG_INF)
    sel = jnp.zeros_like(cand)
    for _ in range(P_TOPK):
        m = jnp.max(work, axis=0, keepdims=True)
        idx = jnp.min(jnp.where(work == m, flat, 1e9), axis=0, keepdims=True)
        hit = flat == idx
        sel = jnp.where(hit, 1.0, sel)
        work = jnp.where(hit, NEG_INF, work)
    z = jnp.sum(sel * gcand, axis=0, keepdims=True)
    counts = [jnp.sum(sel[0:16], axis=0, keepdims=True)]
    for a in range(1, 8):
        counts.append(jnp.sum(sel[8 + 8 * a:16 + 8 * a], axis=0, keepdims=True))
    for a in range(8, 16):
        counts.append(sel[64 + a:65 + a, :])
    ltab = jnp.zeros_like(s1)
    if exact:
        member = rank1 < float(P_TOPK)
        for a in range(P_TOPK):
            ltab = jnp.where(rank1 == float(a), counts[a], ltab)
        bad = None
    else:
        member = s1 >= v1_rows[P_TOPK - 1]
        for a in range(P_TOPK):
            ltab = jnp.where(s1 == v1_rows[a], counts[a], ltab)
        n1 = jnp.sum(jnp.where(member, 1.0, 0.0), axis=0, keepdims=True)
        n2 = jnp.sum(jnp.where(rank2 < float(P_TOPK), 1.0, 0.0), axis=0, keepdims=True)
        bad = jnp.where(n1 + n2 != 2.0 * P_TOPK, 1.0, 0.0)
    atab = jnp.where(member, jnp.exp(s1 - v1_rows[0]) * (1.0 / z), 0.0)
    btab = jnp.exp(s2 - v2_rows[0])
    return ltab, atab, rank2, btab, bad


def _peer_prologue(x_ref, mod_ref, n2g_ref, wqt_ref, keys_ref, ht_scr, qt_scr,
                   rank2_scr, b_scr, l_scr, a_scr, acc_scr, bad_scr):
    tm = x_ref.shape[0]
    ncb = tm // LANES
    nq = wqt_ref.shape[0]
    x = x_ref[...]
    h = _rms(x) * n2g_ref[...] * (1.0 + mod_ref[4:5, :]) + mod_ref[3:4, :]
    htb = h.T.astype(BF16)
    ht_scr[...] = htb
    for half in range(2):
        rs = slice(half * (nq // 2), (half + 1) * (nq // 2))
        qt = _dot(wqt_ref[rs, :], htb)
        for cb in range(ncb):
            qt_scr[cb, rs, :] = qt[:, cb * LANES:(cb + 1) * LANES]
    acc_scr[...] = jnp.zeros_like(acc_scr)

    valid, flat = _cand_index(LANES)

    def scores(hd, cb):
        r1 = pl.multiple_of(hd * (2 * P_HALF), 2 * P_HALF)
        q1 = qt_scr[cb, pl.ds(r1, P_HALF), :].astype(BF16)
        q2 = qt_scr[cb, pl.ds(r1 + P_HALF, P_HALF), :].astype(BF16)
        return _dot(keys_ref[2 * hd], q1), _dot(keys_ref[2 * hd + 1], q2)

    def store(hd, cb, ltab, atab, rank2, btab):
        n = hd * ncb + cb
        l_scr[n] = ltab
        a_scr[n] = atab
        rank2_scr[n] = rank2.astype(rank2_scr.dtype)
        b_scr[n] = btab.astype(b_scr.dtype)

    def head_body(hd, carry):
        bad = jnp.zeros((1, LANES), F32)
        for cb in range(ncb):
            s1, s2 = scores(hd, cb)
            ltab, atab, rank2, btab, bad_cb = _peer_head_tables(s1, s2, valid, flat, exact=False)
            store(hd, cb, ltab, atab, rank2, btab)
            bad = jnp.maximum(bad, bad_cb)
        bad_scr[hd] = jnp.broadcast_to(bad, (SUBLANES, LANES))
        return carry

    lax.fori_loop(0, P_HEADS, head_body, 0)

    def redo_body(hd, carry):
        @pl.when(jnp.max(bad_scr[hd]) > 0.0)
        def _():
            for cb in range(ncb):
                s1, s2 = scores(hd, cb)
                ltab, atab, rank2, btab, _ = _peer_head_tables(s1, s2, valid, flat, exact=True)
                store(hd, cb, ltab, atab, rank2, btab)
        return carry

    lax.fori_loop(0, P_HEADS, redo_body, 0)


PEER_SUB = 512
PEER_VMEM_LIMIT = 56 * 1024 * 1024


def _peer_kernel(x_ref, mod_ref, n2g_ref, wqt_ref, keys_ref, u_ref, vt_ref, o_ref,
                 ht_scr, qt_scr, rank2_scr, b_scr, l_scr, a_scr, acc_scr, bad_scr, act_scr, g_scr):
    j = pl.program_id(1)
    nj = pl.num_programs(1)
    tm = x_ref.shape[0]
    ncb = tm // LANES
    ec = u_ref.shape[0]
    nsub = ec // PEER_SUB
    rows_per_sub = PEER_SUB // N_KEYS

    @pl.when(j == 0)
    def _():
        _peer_prologue(x_ref, mod_ref, n2g_ref, wqt_ref, keys_ref, ht_scr, qt_scr,
                       rank2_scr, b_scr, l_scr, a_scr, acc_scr, bad_scr)

    ht = ht_scr[...]
    for q in range(nsub):
        act_scr[q] = _dot(u_ref[q * PEER_SUB:(q + 1) * PEER_SUB, :], ht)
    for q in range(nsub):
        for r in range(rows_per_sub):
            i1 = j * (ec // N_KEYS) + (q * rows_per_sub + r)
            rs = slice(r * N_KEYS, (r + 1) * N_KEYS)
            for cb in range(ncb):
                cs = slice(cb * LANES, (cb + 1) * LANES)
                w = jnp.zeros((N_KEYS, LANES), BF16)
                for hd in range(P_HEADS):
                    n = hd * ncb + cb
                    lrow = _bcast_row_bf16(l_scr[n, pl.ds(i1, 1), :])
                    arow = _bcast_row_bf16(a_scr[n, pl.ds(i1, 1), :])
                    w = w + jnp.where(rank2_scr[n] < lrow, b_scr[n] * arow, jnp.zeros_like(w))
                g = w.astype(F32) * _gelu(act_scr[q, rs, cs])
                g_scr[q, rs, cs] = g.astype(BF16)
    for q in range(nsub):
        acc_scr[...] += _dot(vt_ref[:, q * PEER_SUB:(q + 1) * PEER_SUB], g_scr[q])

    @pl.when(j == nj - 1)
    def _():
        o_ref[...] = x_ref[...] + mod_ref[5:6, :] * acc_scr[...].T


def _peer(x, mod_l, row_of_tile, n2g, wqt, keys, u_tab, vt_tab, *, tm, ec):
    t, d = x.shape
    nt = t // tm
    ne = u_tab.shape[0]
    ncb = tm // LANES
    nq = wqt.shape[0]
    tabs = lambda dt: pltpu.VMEM((P_HEADS * ncb, N_KEYS, LANES), dt)
    nsub = ec // PEER_SUB
    return pl.pallas_call(
        _peer_kernel,
        grid=(nt, ne // ec),
        in_specs=[pl.BlockSpec((tm, d), lambda i, j: (i, 0)),
                  pl.BlockSpec((None, 6, d), lambda i, j: (row_of_tile(i), 0, 0)),
                  pl.BlockSpec((1, d), lambda i, j: (0, 0)),
                  pl.BlockSpec((nq, d), lambda i, j: (0, 0), pipeline_mode=pl.Buffered(1)),
                  pl.BlockSpec((2 * P_HEADS, N_KEYS, P_HALF), lambda i, j: (0, 0, 0),
                               pipeline_mode=pl.Buffered(1)),
                  pl.BlockSpec((ec, d), lambda i, j: (j, 0)),
                  pl.BlockSpec((d, ec), lambda i, j: (0, j))],
        out_specs=pl.BlockSpec((tm, d), lambda i, j: (i, 0)),
        out_shape=jax.ShapeDtypeStruct((t, d), F32),
        scratch_shapes=[pltpu.VMEM((d, tm), BF16),
                        pltpu.VMEM((ncb, nq, LANES), F32),
                        tabs(BF16), tabs(BF16), tabs(F32), tabs(F32),
                        pltpu.VMEM((d, tm), F32),
                        pltpu.VMEM((P_HEADS, SUBLANES, LANES), F32),
                        pltpu.VMEM((nsub, PEER_SUB, tm), F32),
                        pltpu.VMEM((nsub, PEER_SUB, tm), BF16)],
        compiler_params=_params(("parallel", "arbitrary"), PEER_VMEM_LIMIT),
        name="peer",
    )(x, mod_l, n2g, wqt, keys, u_tab, vt_tab)


def _rope_tables(n_tokens, dim):
    quarter = dim // 4
    pos = np.arange(n_tokens)
    row = (pos // GRID_W).astype(np.float32)
    col = (pos % GRID_W).astype(np.float32)
    lane = np.arange(LANES)
    dd = lane % dim
    half = dd // (dim // 2)
    e = dd % (dim // 2)
    second = e // quarter
    freq = e % quarter
    inv = jnp.asarray(ROPE_BASE, F32) ** (-jnp.arange(quarter, dtype=F32) / quarter)
    inv_lane = inv[freq]
    p = jnp.where(jnp.asarray(half == 0)[None, :], jnp.asarray(row)[:, None], jnp.asarray(col)[:, None])
    ang = p * inv_lane[None, :]
    cos, sin = jnp.cos(ang), jnp.sin(ang)
    first = jnp.asarray(second == 0)[None, :]
    return cos, jnp.where(first, -sin, 0.0), jnp.where(first, 0.0, sin)


def _block_diag_mean(width, group):
    idx = np.arange(width) // group
    return jnp.asarray((idx[:, None] == idx[None, :]).astype(np.float32) / group, BF16)


def kernel(x_prompt, x_sample, cache_a_k, cache_a_v, cache_b_k, cache_b_v, c, c_ctx, ada_w, ada_b, norm1_g, norm2_g, w_in, qn_a, kn_a, qn_b, kn_b, lam_q1, lam_k1, lam_q2, lam_k2, subln_g, out_norm_a, out_norm_c, conv_w, conv_b, w_out, peer_wq, peer_keys, peer_u, peer_v):
    nb, s_p, d = x_prompt.shape
    db, s_l, _ = x_sample.shape
    depth = w_in.shape[0]
    past = cache_a_k.shape[2]
    tm = 256
    tq = 256
    ec = 2048
    assert s_p % tm == 0 and s_l % tm == 0 and s_l % GRID_W == 0 and d == 1024

    n_cond = 1 + db
    n_rows = -(-n_cond // SUBLANES) * SUBLANES
    cond = jnp.concatenate([c_ctx[None, :], c, jnp.zeros((n_rows - n_cond, d), F32)], axis=0)
    mod = _modulation(cond, ada_w, ada_b).reshape(depth, n_rows, 6, d)

    def split_cols(w):
        offs = np.cumsum([0, 384, 128, 128, 384, 384, 384, 256, 256, 256])
        return [w[..., offs[k]:offs[k + 1]] for k in range(9)]

    def swap_kv(w):
        return jnp.concatenate([w[..., 64:128], w[..., 0:64]], axis=-1)

    wqa, wka, wva, wqb, wkb, wvb, wci, wcb, wcc = split_cols(w_in)
    win = jnp.concatenate([wqa, wka, swap_kv(wka), wva, swap_kv(wva), wqb, wkb, wvb, wci, wcb, wcc],
                          axis=-1).astype(BF16)
    wout = w_out.astype(BF16)
    wqt = jnp.swapaxes(peer_wq, 1, 2).astype(BF16)
    keys = peer_keys.reshape(depth, 2 * P_HEADS, N_KEYS, P_HALF).astype(BF16)
    u_tab = peer_u.astype(BF16)
    vt_tab = jnp.swapaxes(peer_v, 1, 2).astype(BF16)

    g64 = _block_diag_mean(384, HD)
    g32 = _block_diag_mean(384, B_DK)
    rope_a = _rope_tables(s_l, HD)
    rope_b = _rope_tables(s_l, B_DK)

    ck = cache_a_k.reshape(db, depth, past, 128)
    cv = cache_a_v.reshape(db, depth, past, 128)
    caches = (jnp.concatenate([ck, swap_kv(ck)], axis=-1).astype(BF16),
              jnp.concatenate([cv, swap_kv(cv)], axis=-1).astype(BF16),
              cache_b_k.reshape(db, depth, past, 384).astype(BF16),
              cache_b_v.reshape(db, depth, past, 384).astype(BF16))

    xp = x_prompt.reshape(nb * s_p, d)
    xs = x_sample.reshape(db * s_l, d)
    tps_p = s_p // tm
    tps_l = s_l // tm
    row_p = lambda i: 0
    row_l = lambda i: 1 + i // tps_l
    tm_peer = 512
    assert s_l % tm_peer == 0 and (nb * s_p) % tm_peer == 0
    row_l_peer = lambda i: 1 + i // (s_l // tm_peer)

    new = [[], [], [], []]
    for l in range(depth):
        lam_init = 0.8 - 0.6 * math.exp(-0.3 * l)
        gains = ((jnp.tile(qn_a[l], A_HEADS) * HD ** -0.5)[None, :],
                 jnp.tile(kn_a[l], 2 * A_KV)[None, :],
                 (jnp.tile(qn_b[l], 2 * B_HEADS) * B_DK ** -0.5)[None, :],
                 jnp.tile(kn_b[l], 2 * B_HEADS)[None, :])
        lams = (lam_q1[l][None, :], lam_k1[l][None, :], lam_q2[l][None, :], lam_k2[l][None, :])
        gb = (jnp.tile(subln_g[l], B_HEADS) * (1.0 - lam_init))[None, :]
        post_args = (out_norm_a[l][None, :], gb, g64, out_norm_c[l][None, :], conv_w[l], conv_b[l][None, :],
                     wout[l])
        n1g = norm1_g[l][None, :]
        n2g = norm2_g[l][None, :]

        qa, ka2, va2, qb, kb, vb, u, cbg, cka, cva, ckb, cvb = _pre_mix(
            xp, mod[l], row_p, n1g, win[l], gains, (g64, g32), None,
            tm=tm, tiles_per_seq=tps_p, emit_cache=True)
        new[0].append(cka.reshape(nb, s_p, A_KV, HD))
        new[1].append(cva.reshape(nb, s_p, A_KV, HD))
        new[2].append(ckb.reshape(nb, s_p, B_HEADS, 2, B_DK))
        new[3].append(cvb.reshape(nb, s_p, B_HEADS, 2 * B_DK))
        r3 = lambda a: a.reshape(nb, s_p, a.shape[-1])
        oa, ob = _attention(r3(qa), r3(ka2), r3(va2), r3(qb), r3(kb), r3(vb), None, lams,
                            layer=l, lam_init=lam_init, tq=min(tq, s_p))
        xp = _post_mix(oa.reshape(nb * s_p, 384), ob.reshape(nb * s_p, 384), u, cbg, xp, mod[l], row_p,
                       *post_args, tm=tm, tiles_per_seq=tps_p)
        xp = _peer(xp, mod[l], row_p, n2g, wqt[l], keys[l], u_tab[l], vt_tab[l], tm=tm_peer, ec=ec)

        qa, ka2, va2, qb, kb, vb, u, cbg = _pre_mix(
            xs, mod[l], row_l, n1g, win[l], gains, (g64, g32), rope_a + rope_b,
            tm=tm, tiles_per_seq=tps_l, emit_cache=False)
        r3 = lambda a: a.reshape(db, s_l, a.shape[-1])
        oa, ob = _attention(r3(qa), r3(ka2), r3(va2), r3(qb), r3(kb), r3(vb), caches, lams,
                            layer=l, lam_init=lam_init, tq=tq)
        xs = _post_mix(oa.reshape(db * s_l, 384), ob.reshape(db * s_l, 384), u, cbg, xs, mod[l], row_l,
                       *post_args, tm=tm, tiles_per_seq=tps_l)
        xs = _peer(xs, mod[l], row_l_peer, n2g, wqt[l], keys[l], u_tab[l], vt_tab[l], tm=tm_peer, ec=ec)

    return (xp.reshape(nb, s_p, d), xs.reshape(db, s_l, d),
            jnp.stack(new[0], axis=1), jnp.stack(new[1], axis=1),
            jnp.stack(new[2], axis=1), jnp.stack(new[3], axis=1))
```

```python
import functools
import math

import numpy as np
import jax
import jax.numpy as jnp
from jax import lax
from jax.experimental import pallas as pl
from jax.experimental.pallas import tpu as pltpu

F32 = jnp.float32
BF16 = jnp.bfloat16

HD = 64
A_HEADS = 6
A_KV = 2
B_HEADS = 6
B_DK = 32
C_WIDTH = 256
A_WIDTH = A_HEADS * HD
B_WIDTH = B_HEADS * 2 * B_DK
GRID_W = 64
ROPE_BASE = 10000.0
EPS = 1e-6
P_HEADS = 8
N_KEYS = 128
P_HALF = 128
P_TOPK = 16

LANES = 128
SUBLANES = 8
VMEM_LIMIT = 48 * 1024 * 1024
TOKEN_TILE = 512

_QA = (0, 384)
_KA2 = (384, 640)
_VA2 = (640, 896)
_QB = (896, 1280)
_KB = (1280, 1664)
_VB = (1664, 2048)
_CI = (2048, 2304)
_CB = (2304, 2560)
_CC = (2560, 2816)
NP_COLS = 2816

NEG_INF = float("-inf")


def _params(sem, vmem=VMEM_LIMIT):
    return pltpu.CompilerParams(dimension_semantics=sem, vmem_limit_bytes=vmem)


def _dot(a, b):
    return jnp.dot(a, b, preferred_element_type=F32)


def _dot_nt(a, b):
    return lax.dot_general(a, b, (((1,), (1,)), ((), ())), preferred_element_type=F32)


def _split3(a):
    hi = a.astype(BF16)
    r = a - hi.astype(F32)
    mid = r.astype(BF16)
    lo = (r - mid.astype(F32)).astype(BF16)
    return hi, mid, lo


def _dot_hp(a, b):
    a0, a1, a2 = _split3(a)
    b0, b1, b2 = _split3(b)
    return (_dot(a0, b0) + (_dot(a0, b1) + _dot(a1, b0))
            + (_dot(a0, b2) + _dot(a1, b1) + _dot(a2, b0)))


def _group_mean_sq(v, g_ref):
    v2 = v * v
    hi = v2.astype(BF16)
    lo = (v2 - hi.astype(F32)).astype(BF16)
    g = g_ref[...]
    return _dot(hi, g) + _dot(lo, g)


def _rms(x):
    return x * lax.rsqrt(jnp.mean(x * x, axis=-1, keepdims=True) + EPS)


def _mod_kernel(cond_ref, w_ref, b_ref, o_ref):
    c = cond_ref[...]
    s = c / (1.0 + jnp.exp(-c))
    o_ref[...] = _dot_hp(s, w_ref[...]) + b_ref[...]


def _modulation(cond, ada_w, ada_b):
    depth, d, d6 = ada_w.shape
    n = cond.shape[0]
    tn = 1536
    return pl.pallas_call(
        _mod_kernel,
        grid=(depth, d6 // tn),
        in_specs=[pl.BlockSpec((n, d), lambda l, j: (0, 0)),
                  pl.BlockSpec((None, d, tn), lambda l, j: (l, 0, j)),
                  pl.BlockSpec((None, 1, tn), lambda l, j: (l, 0, j))],
        out_specs=pl.BlockSpec((None, n, tn), lambda l, j: (l, 0, j)),
        out_shape=jax.ShapeDtypeStruct((depth, n, d6), F32),
        compiler_params=_params(("parallel", "parallel")),
        name="modulation",
    )(cond, ada_w, ada_b.reshape(depth, 1, d6))


def _rope_block(blk, c, sm, sp, shift):
    return (blk * c + pltpu.roll(blk, LANES - shift, 1) * sm + pltpu.roll(blk, shift, 1) * sp)


def _pre_mix_kernel(*refs, rope, emit_cache):
    it = iter(refs)
    x_ref, mod_ref, n1g_ref, win_ref = next(it), next(it), next(it), next(it)
    gqa_ref, gka_ref, gqb_ref, gkb_ref = next(it), next(it), next(it), next(it)
    g64_ref, g32_ref = next(it), next(it)
    if rope:
        ca_ref, sma_ref, spa_ref = next(it), next(it), next(it)
        cb_ref, smb_ref, spb_ref = next(it), next(it), next(it)
    qa_o, ka_o, va_o, qb_o, kb_o, vb_o, u_o, cbg_o = (next(it) for _ in range(8))
    if emit_cache:
        cka_o, cva_o, ckb_o, cvb_o = (next(it) for _ in range(4))

    x = x_ref[...]
    sh1 = mod_ref[0:1, :]
    sc1 = mod_ref[1:2, :]
    h = _rms(x) * n1g_ref[...] * (1.0 + sc1) + sh1
    hb = h.astype(BF16)

    def proj(seg):
        return _dot(hb, win_ref[:, seg[0]:seg[1]])

    def qk(seg, g_ref, gain_ref, width):
        p = proj(seg)
        ms = _group_mean_sq(p, g_ref) if width == 384 else _dot_group256(p, g_ref)
        return p * lax.rsqrt(ms + EPS), gain_ref[...]

    def _dot_group256(p, g_ref):
        v2 = p * p
        hi = v2.astype(BF16)
        lo = (v2 - hi.astype(F32)).astype(BF16)
        g = g_ref[0:256, 0:256]
        return _dot(hi, g) + _dot(lo, g)

    def finish(y, gain, out_ref, tabs, shift, cache_ref=None, cache_cols=None):
        nblk = y.shape[1] // LANES
        for m in range(nblk):
            sl = slice(m * LANES, (m + 1) * LANES)
            blk = y[:, sl] * gain[:, sl]
            if cache_ref is not None and m < cache_cols // LANES:
                cache_ref[:, sl] = blk
            if tabs is not None:
                blk = _rope_block(blk, tabs[0][...], tabs[1][...], tabs[2][...], shift)
            out_ref[:, sl] = blk.astype(out_ref.dtype)

    tabs_a = (ca_ref, sma_ref, spa_ref) if rope else None
    tabs_b = (cb_ref, smb_ref, spb_ref) if rope else None

    y, g = qk(_QA, g64_ref, gqa_ref, 384)
    finish(y, g, qa_o, tabs_a, HD // 4)
    y, g = qk(_KA2, g64_ref, gka_ref, 256)
    finish(y, g, ka_o, tabs_a, HD // 4, cka_o if emit_cache else None, 128)
    va = proj(_VA2)
    va_o[...] = va.astype(va_o.dtype)
    if emit_cache:
        cva_o[...] = va[:, 0:128]
    y, g = qk(_QB, g32_ref, gqb_ref, 384)
    finish(y, g, qb_o, tabs_b, B_DK // 4)
    y, g = qk(_KB, g32_ref, gkb_ref, 384)
    finish(y, g, kb_o, tabs_b, B_DK // 4, ckb_o if emit_cache else None, 384)
    vb = proj(_VB)
    vb_o[...] = vb.astype(vb_o.dtype)
    if emit_cache:
        cvb_o[...] = vb
    u_o[...] = proj(_CC) * proj(_CI)
    cbg_o[...] = proj(_CB)


def _pre_mix(x, mod_l, row_of_tile, n1g, win, gains, gmats, rope_tabs, *, tm, tiles_per_seq, emit_cache):
    t, d = x.shape
    nt = t // tm
    rope = rope_tabs is not None
    const = lambda shape: pl.BlockSpec(shape, lambda i: (0,) * len(shape))
    in_specs = [pl.BlockSpec((tm, d), lambda i: (i, 0)),
                pl.BlockSpec((None, 6, d), lambda i: (row_of_tile(i), 0, 0)),
                const((1, d)), const((d, NP_COLS)),
                const((1, 384)), const((1, 256)), const((1, 384)), const((1, 384)),
                const((384, 384)), const((384, 384))]
    args = [x, mod_l, n1g, win, *gains, *gmats]
    if rope:
        in_specs += [pl.BlockSpec((tm, LANES), lambda i: (i % tiles_per_seq, 0))] * 6
        args += list(rope_tabs)
    widths = [(384, BF16), (256, BF16), (256, BF16), (384, BF16), (384, BF16), (384, BF16),
              (256, F32), (256, F32)]
    if emit_cache:
        widths += [(128, F32), (128, F32), (384, F32), (384, F32)]
    out_shape = [jax.ShapeDtypeStruct((t, w), dt) for w, dt in widths]
    out_specs = [pl.BlockSpec((tm, w), lambda i: (i, 0)) for w, _ in widths]
    return pl.pallas_call(
        functools.partial(_pre_mix_kernel, rope=rope, emit_cache=emit_cache),
        grid=(nt,), in_specs=in_specs, out_specs=out_specs, out_shape=out_shape,
        compiler_params=_params(("parallel",)),
        name="pre_mix_rope" if rope else "pre_mix",
    )(*args)


def _softmax_pv(q, k_parts, v_parts):
    ss = [_dot_nt(q, k) for k in k_parts]
    m = jnp.max(ss[0], axis=-1, keepdims=True)
    for s in ss[1:]:
        m = jnp.maximum(m, jnp.max(s, axis=-1, keepdims=True))
    l = None
    o = None
    for s, v in zip(ss, v_parts):
        p = jnp.exp(s - m)
        ls = jnp.sum(p, axis=-1, keepdims=True)
        os_ = _dot(p.astype(BF16), v)
        l = ls if l is None else l + ls
        o = os_ if o is None else o + os_
    return o * (1.0 / l)


def _attn_kernel(*refs, has_cache, lam_init):
    it = iter(refs)
    qa_ref, ka_ref, va_ref, qb_ref, kb_ref, vb_ref = (next(it) for _ in range(6))
    if has_cache:
        cka_ref, cva_ref, ckb_ref, cvb_ref = (next(it) for _ in range(4))
    lq1_ref, lk1_ref, lq2_ref, lk2_ref = (next(it) for _ in range(4))
    oa_ref, ob_ref = next(it), next(it)

    tq = qa_ref.shape[0]
    lane = lax.broadcasted_iota(jnp.int32, (tq, LANES), 1)
    lo_half = lane < 64

    lam = (jnp.exp(jnp.sum(lq1_ref[...] * lk1_ref[...], axis=-1, keepdims=True))
           - jnp.exp(jnp.sum(lq2_ref[...] * lk2_ref[...], axis=-1, keepdims=True)) + lam_init)

    for m in range(A_HEADS // 2):
        qblk = qa_ref[:, m * LANES:(m + 1) * LANES]
        outs = []
        for par in range(2):
            kv = (2 * m + par) // (A_HEADS // A_KV)
            kblk = 0 if kv == par else 1
            ksl = slice(kblk * LANES, (kblk + 1) * LANES)
            qh = jnp.where(lo_half if par == 0 else jnp.logical_not(lo_half), qblk, jnp.zeros_like(qblk))
            ks, vs = [ka_ref[:, ksl]], [va_ref[:, ksl]]
            if has_cache:
                ks.append(cka_ref[:, ksl])
                vs.append(cva_ref[:, ksl])
            outs.append(_softmax_pv(qh, ks, vs))
        oa_ref[:, m * LANES:(m + 1) * LANES] = jnp.where(lo_half, outs[0], outs[1])

    seg_id = lane >> 5
    for m in range(B_HEADS // 2):
        sl = slice(m * LANES, (m + 1) * LANES)
        qblk = qb_ref[:, sl]
        ks, vs = [kb_ref[:, sl]], [vb_ref[:, sl]]
        if has_cache:
            ks.append(ckb_ref[:, sl])
            vs.append(cvb_ref[:, sl])
        outs = []
        for par in range(2):
            maps = []
            for j in range(2):
                qs = jnp.where(seg_id == (2 * par + j), qblk, jnp.zeros_like(qblk))
                maps.append(_softmax_pv(qs, ks, vs))
            outs.append(maps[0] - lam * maps[1])
        ob_ref[:, sl] = jnp.where(lo_half, outs[0], outs[1])


def _attention(qa, ka2, va2, qb, kb, vb, caches, lams, *, layer, lam_init, tq):
    b, s, _ = qa.shape
    has_cache = caches is not None
    qspec = lambda w: pl.BlockSpec((None, tq, w), lambda i, j: (i, j, 0))
    kspec = lambda w: pl.BlockSpec((None, s, w), lambda i, j: (i, 0, 0))
    in_specs = [qspec(384), kspec(256), kspec(256), qspec(384), kspec(384), kspec(384)]
    args = [qa, ka2, va2, qb, kb, vb]
    if has_cache:
        lc = caches[0].shape[2]
        cspec = lambda w: pl.BlockSpec((None, None, lc, w), lambda i, j: (i, layer, 0, 0))
        in_specs += [cspec(256), cspec(256), cspec(384), cspec(384)]
        args += list(caches)
    in_specs += [pl.BlockSpec((1, B_DK), lambda i, j: (0, 0))] * 4
    args += list(lams)
    return pl.pallas_call(
        functools.partial(_attn_kernel, has_cache=has_cache, lam_init=lam_init),
        grid=(b, s // tq), in_specs=in_specs,
        out_specs=[qspec(384), qspec(384)],
        out_shape=[jax.ShapeDtypeStruct((b, s, 384), F32)] * 2,
        compiler_params=_params(("parallel", "parallel")),
        name="attention_cached" if has_cache else "attention",
    )(*args)


def _post_mix_kernel(oa_ref, ob_ref, u_ref, up_ref, un_ref, cbg_ref, x_ref, mod_ref,
                     ga_ref, gb_ref, g64_ref, gc_ref, cw_ref, cbias_ref, wout_ref, o_ref,
                     *, tiles_per_seq):
    i = pl.program_id(0)
    tm = x_ref.shape[0]
    pos = i % tiles_per_seq
    u = u_ref[...]
    row = lax.broadcasted_iota(jnp.int32, u.shape, 0)
    prev_row = jnp.where(pos == 0, 0.0, 1.0) * up_ref[SUBLANES - 1:SUBLANES, :]
    next_row = jnp.where(pos == tiles_per_seq - 1, 0.0, 1.0) * un_ref[0:1, :]
    u_dn = jnp.where(row == 0, prev_row, pltpu.roll(u, 1, 0))
    u_up = jnp.where(row == tm - 1, next_row, pltpu.roll(u, tm - 1, 0))
    y = u_dn * cw_ref[0:1, :] + u * cw_ref[1:2, :] + u_up * cw_ref[2:3, :] + cbias_ref[...]
    oc = cbg_ref[...] * y

    na = _rms(oa_ref[...]) * ga_ref[...]
    ob = ob_ref[...]
    nb = ob * lax.rsqrt(_group_mean_sq(ob, g64_ref) + EPS) * gb_ref[...]
    nc = _rms(oc) * gc_ref[...]
    mix = (_dot(na.astype(BF16), wout_ref[0:384, :]) + _dot(nb.astype(BF16), wout_ref[384:768, :])
           + _dot(nc.astype(BF16), wout_ref[768:1024, :]))
    o_ref[...] = x_ref[...] + mod_ref[2:3, :] * mix


def _post_mix(oa, ob, u, cbg, x, mod_l, row_of_tile, ga, gb, g64, gc, cw, cbias, wout, *, tm, tiles_per_seq):
    t, d = x.shape
    nt = t // tm
    nb8 = t // SUBLANES
    step = tm // SUBLANES
    const = lambda shape: pl.BlockSpec(shape, lambda i: (0,) * len(shape))
    tok = lambda w: pl.BlockSpec((tm, w), lambda i: (i, 0))
    in_specs = [tok(384), tok(384), tok(256),
                pl.BlockSpec((SUBLANES, 256), lambda i: (jnp.maximum(i * step - 1, 0), 0)),
                pl.BlockSpec((SUBLANES, 256), lambda i: (jnp.minimum((i + 1) * step, nb8 - 1), 0)),
                tok(256), tok(d),
                pl.BlockSpec((None, 6, d), lambda i: (row_of_tile(i), 0, 0)),
                const((1, 384)), const((1, 384)), const((384, 384)), const((1, 256)),
                const((3, 256)), const((1, 256)), const((d, d))]
    return pl.pallas_call(
        functools.partial(_post_mix_kernel, tiles_per_seq=tiles_per_seq),
        grid=(nt,), in_specs=in_specs, out_specs=tok(d),
        out_shape=jax.ShapeDtypeStruct((t, d), F32),
        compiler_params=_params(("parallel",)),
        name="post_mix",
    )(oa, ob, u, u, u, cbg, x, mod_l, ga, gb, g64, gc, cw, cbias, wout)


def _gelu(x):
    k1 = -2.0 * math.sqrt(2.0 / math.pi) * math.log2(math.e)
    return x / (1.0 + jnp.exp2(x * (k1 + (k1 * 0.044715) * (x * x))))


BF16_SUBLANES = 16


def _bcast_row_bf16(row):
    packed = jnp.broadcast_to(row, (BF16_SUBLANES, LANES)).astype(BF16)
    return jnp.concatenate([packed] * (N_KEYS // BF16_SUBLANES), axis=0)


def _top16(s):
    n = s.shape[0]
    rows = lax.broadcasted_iota(jnp.int32, s.shape, 0).astype(F32)
    work = s
    rank = jnp.full(s.shape, float(P_TOPK), F32)
    vals = []
    for a in range(P_TOPK):
        m = jnp.max(work, axis=0, keepdims=True)
        idx = jnp.min(jnp.where(work == m, rows, float(n)), axis=0, keepdims=True)
        hit = rows == idx
        rank = jnp.where(hit, float(a), rank)
        work = jnp.where(hit, NEG_INF, work)
        vals.append(m)
    return vals, rank


def _top16_distinct(s, want_rank):
    work = s
    rank = jnp.full(s.shape, float(P_TOPK), F32) if want_rank else None
    vals = []
    for a in range(P_TOPK):
        m = jnp.max(work, axis=0, keepdims=True)
        hit = work == m
        if want_rank:
            rank = jnp.where(hit, float(a), rank)
        work = jnp.where(hit, NEG_INF, work)
        vals.append(m)
    return vals, rank


def _stack16(rows_list):
    w = rows_list[0].shape[1]
    r = lax.broadcasted_iota(jnp.int32, (P_TOPK, w), 0)
    out = jnp.zeros((P_TOPK, w), F32)
    for a, v in enumerate(rows_list):
        out = jnp.where(r == a, v, out)
    return out


_CAND_ROWS = 80


def _cand_build(v1_rows, v1_arr, v2_arr, v2_row0, op):
    pieces = [op(v1_rows[0], v2_arr), op(v1_rows[1], v2_arr[0:8])]
    for a in range(2, 8):
        pieces.append(op(v1_rows[a], v2_arr[0:8]))
    pieces.append(op(v1_arr[8:16], v2_row0))
    return jnp.concatenate(pieces, axis=0)


def _cand_index(w):
    row = lax.broadcasted_iota(jnp.int32, (_CAND_ROWS, w), 0)
    a = jnp.where(row < 16, 0, jnp.where(row < 72, 1 + ((row - 16) >> 3), 8 + (row - 72)))
    b = jnp.where(row < 16, row, jnp.where(row < 72, (row - 16) & 7, 0))
    valid = (a + 1) * (b + 1) <= P_TOPK
    flat = (a * P_TOPK + b).astype(F32)
    return valid, flat


def _peer_head_tables(s1, s2, valid, flat, exact):
    if exact:
        v1_rows, rank1 = _top16(s1)
        v2_rows, rank2 = _top16(s2)
    else:
        v1_rows, _ = _top16_distinct(s1, False)
        v2_rows, rank2 = _top16_distinct(s2, True)
    v1 = _stack16(v1_rows)
    v2 = _stack16(v2_rows)
    e1 = jnp.exp(v1 - v1_rows[0])
    e2 = jnp.exp(v2 - v2_rows[0])
    e1_rows = [e1[a:a + 1, :] for a in range(8)]
    cand = _cand_build(v1_rows, v1, v2, v2_rows[0], lambda p, q: p + q)
    gcand = _cand_build(e1_rows, e1, e2, e2[0:1, :], lambda p, q: p * q)
    work = jnp.where(valid, cand, NEG_INF)
    sel = jnp.zeros_like(cand)
    for _ in range(P_TOPK):
        m = jnp.max(work, axis=0, keepdims=True)
        idx = jnp.min(jnp.where(work == m, flat, 1e9), axis=0, keepdims=True)
        hit = flat == idx
        sel = jnp.where(hit, 1.0, sel)
        work = jnp.where(hit, NEG_INF, work)
    z = jnp.sum(sel * gcand, axis=0, keepdims=True)
    counts = [jnp.sum(sel[0:16], axis=0, keepdims=True)]
    for a in range(1, 8):
        counts.append(jnp.sum(sel[8 + 8 * a:16 + 8 * a], axis=0, keepdims=True))
    for a in range(8, 16):
        counts.append(sel[64 + a:65 + a, :])
    ltab = jnp.zeros_like(s1)
    if exact:
        member = rank1 < float(P_TOPK)
        for a in range(P_TOPK):
            ltab = jnp.where(rank1 == float(a), counts[a], ltab)
        bad = None
    else:
        member = s1 >= v1_rows[P_TOPK - 1]
        for a in range(P_TOPK):
            ltab = jnp.where(s1 == v1_rows[a], counts[a], ltab)
        n1 = jnp.sum(jnp.where(member, 1.0, 0.0), axis=0, keepdims=True)
        n2 = jnp.sum(jnp.where(rank2 < float(P_TOPK), 1.0, 0.0), axis=0, keepdims=True)
        bad = jnp.where(n1 + n2 != 2.0 * P_TOPK, 1.0, 0.0)
    atab = jnp.where(member, jnp.exp(s1 - v1_rows[0]) * (1.0 / z), 0.0)
    btab = jnp.exp(s2 - v2_rows[0])
    return ltab, atab, rank2, btab, bad


def _peer_prologue(x_ref, mod_ref, n2g_ref, wqt_ref, keys_ref, ht_scr, qt_scr,
                   rank2_scr, b_scr, l_scr, a_scr, acc_scr, bad_scr):
    tm = x_ref.shape[0]
    ncb = tm // LANES
    nq = wqt_ref.shape[0]
    x = x_ref[...]
    h = _rms(x) * n2g_ref[...] * (1.0 + mod_ref[4:5, :]) + mod_ref[3:4, :]
    htb = h.T.astype(BF16)
    ht_scr[...] = htb
    for half in range(2):
        rs = slice(half * (nq // 2), (half + 1) * (nq // 2))
        qt = _dot(wqt_ref[rs, :], htb)
        for cb in range(ncb):
            qt_scr[cb, rs, :] = qt[:, cb * LANES:(cb + 1) * LANES]
    acc_scr[...] = jnp.zeros_like(acc_scr)

    valid, flat = _cand_index(LANES)

    def scores(hd, cb):
        r1 = pl.multiple_of(hd * (2 * P_HALF), 2 * P_HALF)
        q1 = qt_scr[cb, pl.ds(r1, P_HALF), :].astype(BF16)
        q2 = qt_scr[cb, pl.ds(r1 + P_HALF, P_HALF), :].astype(BF16)
        return _dot(keys_ref[2 * hd], q1), _dot(keys_ref[2 * hd + 1], q2)

    def store(hd, cb, ltab, atab, rank2, btab):
        n = hd * ncb + cb
        l_scr[n] = ltab
        a_scr[n] = atab
        rank2_scr[n] = rank2.astype(rank2_scr.dtype)
        b_scr[n] = btab.astype(b_scr.dtype)

    def head_body(hd, carry):
        bad = jnp.zeros((1, LANES), F32)
        for cb in range(ncb):
            s1, s2 = scores(hd, cb)
            ltab, atab, rank2, btab, bad_cb = _peer_head_tables(s1, s2, valid, flat, exact=False)
            store(hd, cb, ltab, atab, rank2, btab)
            bad = jnp.maximum(bad, bad_cb)
        bad_scr[hd] = jnp.broadcast_to(bad, (SUBLANES, LANES))
        return carry

    lax.fori_loop(0, P_HEADS, head_body, 0)

    def redo_body(hd, carry):
        @pl.when(jnp.max(bad_scr[hd]) > 0.0)
        def _():
            for cb in range(ncb):
                s1, s2 = scores(hd, cb)
                ltab, atab, rank2, btab, _ = _peer_head_tables(s1, s2, valid, flat, exact=True)
                store(hd, cb, ltab, atab, rank2, btab)
        return carry

    lax.fori_loop(0, P_HEADS, redo_body, 0)


PEER_SUB = 512
MXU_COLS = 256
PEER_VMEM_LIMIT = 56 * 1024 * 1024


def _peer_kernel(x_ref, mod_ref, n2g_ref, wqt_ref, keys_ref, u_ref, vt_ref, o_ref,
                 ht_scr, qt_scr, rank2_scr, b_scr, l_scr, a_scr, acc_scr, bad_scr, act_scr, g_scr):
    j = pl.program_id(1)
    nj = pl.num_programs(1)
    tm = x_ref.shape[0]
    ncb = tm // LANES
    ec = u_ref.shape[0]
    nsub = ec // PEER_SUB
    rows_per_sub = PEER_SUB // N_KEYS

    @pl.when(j == 0)
    def _():
        _peer_prologue(x_ref, mod_ref, n2g_ref, wqt_ref, keys_ref, ht_scr, qt_scr,
                       rank2_scr, b_scr, l_scr, a_scr, acc_scr, bad_scr)

    for nb in range(tm // MXU_COLS):
        ns = slice(nb * MXU_COLS, (nb + 1) * MXU_COLS)
        act_scr[:, ns] = _dot(u_ref[...], ht_scr[:, ns])

    def gate_body(q, carry):
        for cb in range(ncb):
            cs = slice(cb * LANES, (cb + 1) * LANES)
            ws = [jnp.zeros((N_KEYS, LANES), BF16) for _ in range(rows_per_sub)]
            for hd in range(P_HEADS):
                n = hd * ncb + cb
                rank2 = rank2_scr[n]
                btab = b_scr[n]
                for r in range(rows_per_sub):
                    i1 = j * (ec // N_KEYS) + q * rows_per_sub + r
                    lrow = _bcast_row_bf16(l_scr[n, pl.ds(i1, 1), :])
                    arow = _bcast_row_bf16(a_scr[n, pl.ds(i1, 1), :])
                    ws[r] = ws[r] + jnp.where(rank2 < lrow, btab * arow, jnp.zeros_like(btab))
            for r in range(rows_per_sub):
                rs = pl.ds(pl.multiple_of(q * PEER_SUB + r * N_KEYS, N_KEYS), N_KEYS)
                g = ws[r] * _gelu(act_scr[rs, cs].astype(BF16))
                g_scr[rs, cs] = g
        return carry

    lax.fori_loop(0, nsub, gate_body, 0)
    for nb in range(tm // MXU_COLS):
        ns = slice(nb * MXU_COLS, (nb + 1) * MXU_COLS)
        acc_scr[:, ns] += _dot(vt_ref[...], g_scr[:, ns])

    @pl.when(j == nj - 1)
    def _():
        o_ref[...] = x_ref[...] + mod_ref[5:6, :] * acc_scr[...].T


def _peer(x, mod_l, row_of_tile, n2g, wqt, keys, u_tab, vt_tab, *, tm, ec):
    t, d = x.shape
    nt = t // tm
    ne = u_tab.shape[0]
    ncb = tm // LANES
    nq = wqt.shape[0]
    tabs = lambda dt: pltpu.VMEM((P_HEADS * ncb, N_KEYS, LANES), dt)
    return pl.pallas_call(
        _peer_kernel,
        grid=(nt, ne // ec),
        in_specs=[pl.BlockSpec((tm, d), lambda i, j: (i, 0)),
                  pl.BlockSpec((None, 6, d), lambda i, j: (row_of_tile(i), 0, 0)),
                  pl.BlockSpec((1, d), lambda i, j: (0, 0)),
                  pl.BlockSpec((nq, d), lambda i, j: (0, 0), pipeline_mode=pl.Buffered(1)),
                  pl.BlockSpec((2 * P_HEADS, N_KEYS, P_HALF), lambda i, j: (0, 0, 0),
                               pipeline_mode=pl.Buffered(1)),
                  pl.BlockSpec((ec, d), lambda i, j: (j, 0)),
                  pl.BlockSpec((d, ec), lambda i, j: (0, j))],
        out_specs=pl.BlockSpec((tm, d), lambda i, j: (i, 0)),
        out_shape=jax.ShapeDtypeStruct((t, d), F32),
        scratch_shapes=[pltpu.VMEM((d, tm), BF16),
                        pltpu.VMEM((ncb, nq, LANES), F32),
                        tabs(BF16), tabs(BF16), tabs(F32), tabs(F32),
                        pltpu.VMEM((d, tm), F32),
                        pltpu.VMEM((P_HEADS, SUBLANES, LANES), F32),
                        pltpu.VMEM((ec, tm), F32),
                        pltpu.VMEM((ec, tm), BF16)],
        compiler_params=_params(("parallel", "arbitrary"), PEER_VMEM_LIMIT),
        name="peer",
    )(x, mod_l, n2g, wqt, keys, u_tab, vt_tab)


def _rope_tables(n_tokens, dim):
    quarter = dim // 4
    pos = np.arange(n_tokens)
    row = (pos // GRID_W).astype(np.float32)
    col = (pos % GRID_W).astype(np.float32)
    lane = np.arange(LANES)
    dd = lane % dim
    half = dd // (dim // 2)
    e = dd % (dim // 2)
    second = e // quarter
    freq = e % quarter
    inv = jnp.asarray(ROPE_BASE, F32) ** (-jnp.arange(quarter, dtype=F32) / quarter)
    inv_lane = inv[freq]
    p = jnp.where(jnp.asarray(half == 0)[None, :], jnp.asarray(row)[:, None], jnp.asarray(col)[:, None])
    ang = p * inv_lane[None, :]
    cos, sin = jnp.cos(ang), jnp.sin(ang)
    first = jnp.asarray(second == 0)[None, :]
    return cos, jnp.where(first, -sin, 0.0), jnp.where(first, 0.0, sin)


def _block_diag_mean(width, group):
    idx = np.arange(width) // group
    return jnp.asarray((idx[:, None] == idx[None, :]).astype(np.float32) / group, BF16)


def kernel(x_prompt, x_sample, cache_a_k, cache_a_v, cache_b_k, cache_b_v, c, c_ctx, ada_w, ada_b, norm1_g, norm2_g, w_in, qn_a, kn_a, qn_b, kn_b, lam_q1, lam_k1, lam_q2, lam_k2, subln_g, out_norm_a, out_norm_c, conv_w, conv_b, w_out, peer_wq, peer_keys, peer_u, peer_v):
    nb, s_p, d = x_prompt.shape
    db, s_l, _ = x_sample.shape
    depth = w_in.shape[0]
    past = cache_a_k.shape[2]
    tm_p = s_p
    tm_l = min(TOKEN_TILE, s_l)
    tq_l = min(256, s_l)
    tm_pre_p = min(TOKEN_TILE, nb * s_p)
    tm_peer = TOKEN_TILE
    ec = 2048
    assert s_l % tm_l == 0 and s_l % GRID_W == 0 and d == 1024 and (nb * s_p) % tm_pre_p == 0
    assert s_l % tm_peer == 0 and (nb * s_p) % tm_peer == 0 and tm_p % SUBLANES == 0

    n_cond = 1 + db
    n_rows = -(-n_cond // SUBLANES) * SUBLANES
    cond = jnp.concatenate([c_ctx[None, :], c, jnp.zeros((n_rows - n_cond, d), F32)], axis=0)
    mod = _modulation(cond, ada_w, ada_b).reshape(depth, n_rows, 6, d)

    def split_cols(w):
        offs = np.cumsum([0, 384, 128, 128, 384, 384, 384, 256, 256, 256])
        return [w[..., offs[k]:offs[k + 1]] for k in range(9)]

    def swap_kv(w):
        return jnp.concatenate([w[..., 64:128], w[..., 0:64]], axis=-1)

    wqa, wka, wva, wqb, wkb, wvb, wci, wcb, wcc = split_cols(w_in)
    win = jnp.concatenate([wqa, wka, swap_kv(wka), wva, swap_kv(wva), wqb, wkb, wvb, wci, wcb, wcc],
                          axis=-1).astype(BF16)
    wout = w_out.astype(BF16)
    wqt = jnp.swapaxes(peer_wq, 1, 2).astype(BF16)
    keys = peer_keys.reshape(depth, 2 * P_HEADS, N_KEYS, P_HALF).astype(BF16)
    u_tab = peer_u.astype(BF16)
    vt_tab = jnp.swapaxes(peer_v, 1, 2).astype(BF16)

    g64 = _block_diag_mean(384, HD)
    g32 = _block_diag_mean(384, B_DK)
    rope_a = _rope_tables(s_l, HD)
    rope_b = _rope_tables(s_l, B_DK)

    ck = cache_a_k.reshape(db, depth, past, 128)
    cv = cache_a_v.reshape(db, depth, past, 128)
    caches = (jnp.concatenate([ck, swap_kv(ck)], axis=-1).astype(BF16),
              jnp.concatenate([cv, swap_kv(cv)], axis=-1).astype(BF16),
              cache_b_k.reshape(db, depth, past, 384).astype(BF16),
              cache_b_v.reshape(db, depth, past, 384).astype(BF16))

    xp = x_prompt.reshape(nb * s_p, d)
    xs = x_sample.reshape(db * s_l, d)
    tps_p = s_p // tm_p
    tps_l = s_l // tm_l
    row_p = lambda i: 0
    row_l = lambda i: 1 + i // tps_l
    row_l_peer = lambda i: 1 + i // (s_l // tm_peer)

    new = [[], [], [], []]
    for l in range(depth):
        lam_init = 0.8 - 0.6 * math.exp(-0.3 * l)
        gains = ((jnp.tile(qn_a[l], A_HEADS) * HD ** -0.5)[None, :],
                 jnp.tile(kn_a[l], 2 * A_KV)[None, :],
                 (jnp.tile(qn_b[l], 2 * B_HEADS) * B_DK ** -0.5)[None, :],
                 jnp.tile(kn_b[l], 2 * B_HEADS)[None, :])
        lams = (lam_q1[l][None, :], lam_k1[l][None, :], lam_q2[l][None, :], lam_k2[l][None, :])
        gb = (jnp.tile(subln_g[l], B_HEADS) * (1.0 - lam_init))[None, :]
        post_args = (out_norm_a[l][None, :], gb, g64, out_norm_c[l][None, :], conv_w[l], conv_b[l][None, :],
                     wout[l])
        n1g = norm1_g[l][None, :]
        n2g = norm2_g[l][None, :]

        qa, ka2, va2, qb, kb, vb, u, cbg, cka, cva, ckb, cvb = _pre_mix(
            xp, mod[l], row_p, n1g, win[l], gains, (g64, g32), None,
            tm=tm_pre_p, tiles_per_seq=1, emit_cache=True)
        new[0].append(cka.reshape(nb, s_p, A_KV, HD))
        new[1].append(cva.reshape(nb, s_p, A_KV, HD))
        new[2].append(ckb.reshape(nb, s_p, B_HEADS, 2, B_DK))
        new[3].append(cvb.reshape(nb, s_p, B_HEADS, 2 * B_DK))
        r3 = lambda a: a.reshape(nb, s_p, a.shape[-1])
        oa, ob = _attention(r3(qa), r3(ka2), r3(va2), r3(qb), r3(kb), r3(vb), None, lams,
                            layer=l, lam_init=lam_init, tq=s_p)
        xp = _post_mix(oa.reshape(nb * s_p, 384), ob.reshape(nb * s_p, 384), u, cbg, xp, mod[l], row_p,
                       *post_args, tm=tm_p, tiles_per_seq=tps_p)
        xp = _peer(xp, mod[l], row_p, n2g, wqt[l], keys[l], u_tab[l], vt_tab[l], tm=tm_peer, ec=ec)

        qa, ka2, va2, qb, kb, vb, u, cbg = _pre_mix(
            xs, mod[l], row_l, n1g, win[l], gains, (g64, g32), rope_a + rope_b,
            tm=tm_l, tiles_per_seq=tps_l, emit_cache=False)
        r3 = lambda a: a.reshape(db, s_l, a.shape[-1])
        oa, ob = _attention(r3(qa), r3(ka2), r3(va2), r3(qb), r3(kb), r3(vb), caches, lams,
                            layer=l, lam_init=lam_init, tq=tq_l)
        xs = _post_mix(oa.reshape(db * s_l, 384), ob.reshape(db * s_l, 384), u, cbg, xs, mod[l], row_l,
                       *post_args, tm=tm_l, tiles_per_seq=tps_l)
        xs = _peer(xs, mod[l], row_l_peer, n2g, wqt[l], keys[l], u_tab[l], vt_tab[l], tm=tm_peer, ec=ec)

    return (xp.reshape(nb, s_p, d), xs.reshape(db, s_l, d),
            jnp.stack(new[0], axis=1), jnp.stack(new[1], axis=1),
            jnp.stack(new[2], axis=1), jnp.stack(new[3], axis=1))
```

```python
import functools
import math

import numpy as np
import jax
import jax.numpy as jnp
from jax import lax
from jax.experimental import pallas as pl
from jax.experimental.pallas import tpu as pltpu

F32 = jnp.float32
BF16 = jnp.bfloat16

HD = 64
A_HEADS = 6
A_KV = 2
B_HEADS = 6
B_DK = 32
C_WIDTH = 256
A_WIDTH = A_HEADS * HD
B_WIDTH = B_HEADS * 2 * B_DK
GRID_W = 64
ROPE_BASE = 10000.0
EPS = 1e-6
P_HEADS = 8
N_KEYS = 128
P_HALF = 128
P_TOPK = 16

LANES = 128
SUBLANES = 8
VMEM_LIMIT = 48 * 1024 * 1024
TOKEN_TILE = 512
MIX_TILE = 1024

_QA = (0, 384)
_KA2 = (384, 640)
_VA2 = (640, 896)
_QB = (896, 1280)
_KB = (1280, 1664)
_VB = (1664, 2048)
_CI = (2048, 2304)
_CB = (2304, 2560)
_CC = (2560, 2816)
NP_COLS = 2816

NEG_INF = float("-inf")


def _params(sem, vmem=VMEM_LIMIT):
    return pltpu.CompilerParams(dimension_semantics=sem, vmem_limit_bytes=vmem)


def _dot(a, b):
    return jnp.dot(a, b, preferred_element_type=F32)


def _dot_nt(a, b):
    return lax.dot_general(a, b, (((1,), (1,)), ((), ())), preferred_element_type=F32)


def _split3(a):
    hi = a.astype(BF16)
    r = a - hi.astype(F32)
    mid = r.astype(BF16)
    lo = (r - mid.astype(F32)).astype(BF16)
    return hi, mid, lo


def _dot_hp(a, b):
    a0, a1, a2 = _split3(a)
    b0, b1, b2 = _split3(b)
    return (_dot(a0, b0) + (_dot(a0, b1) + _dot(a1, b0))
            + (_dot(a0, b2) + _dot(a1, b1) + _dot(a2, b0)))


def _group_mean_sq(v, g_ref):
    v2 = v * v
    hi = v2.astype(BF16)
    lo = (v2 - hi.astype(F32)).astype(BF16)
    g = g_ref[...]
    return _dot(hi, g) + _dot(lo, g)


def _rms(x):
    return x * lax.rsqrt(jnp.mean(x * x, axis=-1, keepdims=True) + EPS)


def _mod_kernel(cond_ref, w_ref, b_ref, o_ref):
    c = cond_ref[...]
    s = c / (1.0 + jnp.exp(-c))
    o_ref[...] = _dot_hp(s, w_ref[...]) + b_ref[...]


def _modulation(cond, ada_w, ada_b):
    depth, d, d6 = ada_w.shape
    n = cond.shape[0]
    tn = 1536
    return pl.pallas_call(
        _mod_kernel,
        grid=(depth, d6 // tn),
        in_specs=[pl.BlockSpec((n, d), lambda l, j: (0, 0)),
                  pl.BlockSpec((None, d, tn), lambda l, j: (l, 0, j)),
                  pl.BlockSpec((None, 1, tn), lambda l, j: (l, 0, j))],
        out_specs=pl.BlockSpec((None, n, tn), lambda l, j: (l, 0, j)),
        out_shape=jax.ShapeDtypeStruct((depth, n, d6), F32),
        compiler_params=_params(("parallel", "parallel")),
        name="modulation",
    )(cond, ada_w, ada_b.reshape(depth, 1, d6))


def _rope_block(blk, c, sm, sp, shift):
    return (blk * c + pltpu.roll(blk, LANES - shift, 1) * sm + pltpu.roll(blk, shift, 1) * sp)


def _pre_mix_kernel(*refs, rope, emit_cache):
    it = iter(refs)
    x_ref, mod_ref, n1g_ref, win_ref = next(it), next(it), next(it), next(it)
    gqa_ref, gka_ref, gqb_ref, gkb_ref = next(it), next(it), next(it), next(it)
    g64_ref, g32_ref = next(it), next(it)
    if rope:
        ca_ref, sma_ref, spa_ref = next(it), next(it), next(it)
        cb_ref, smb_ref, spb_ref = next(it), next(it), next(it)
    qa_o, ka_o, va_o, qb_o, kb_o, vb_o, u_o, cbg_o = (next(it) for _ in range(8))
    if emit_cache:
        cka_o, cva_o, ckb_o, cvb_o = (next(it) for _ in range(4))

    x = x_ref[...]
    sh1 = mod_ref[0:1, :]
    sc1 = mod_ref[1:2, :]
    h = _rms(x) * n1g_ref[...] * (1.0 + sc1) + sh1
    hb = h.astype(BF16)

    def proj(seg):
        return _dot(hb, win_ref[:, seg[0]:seg[1]])

    def qk(seg, g_ref, gain_ref, width):
        p = proj(seg)
        ms = _group_mean_sq(p, g_ref) if width == 384 else _dot_group256(p, g_ref)
        return p * lax.rsqrt(ms + EPS), gain_ref[...]

    def _dot_group256(p, g_ref):
        v2 = p * p
        hi = v2.astype(BF16)
        lo = (v2 - hi.astype(F32)).astype(BF16)
        g = g_ref[0:256, 0:256]
        return _dot(hi, g) + _dot(lo, g)

    def finish(y, gain, out_ref, tabs, shift, cache_ref=None, cache_cols=None):
        nblk = y.shape[1] // LANES
        for m in range(nblk):
            sl = slice(m * LANES, (m + 1) * LANES)
            blk = y[:, sl] * gain[:, sl]
            if cache_ref is not None and m < cache_cols // LANES:
                cache_ref[:, sl] = blk
            if tabs is not None:
                blk = _rope_block(blk, tabs[0][...], tabs[1][...], tabs[2][...], shift)
            out_ref[:, sl] = blk.astype(out_ref.dtype)

    tabs_a = (ca_ref, sma_ref, spa_ref) if rope else None
    tabs_b = (cb_ref, smb_ref, spb_ref) if rope else None

    y, g = qk(_QA, g64_ref, gqa_ref, 384)
    finish(y, g, qa_o, tabs_a, HD // 4)
    y, g = qk(_KA2, g64_ref, gka_ref, 256)
    finish(y, g, ka_o, tabs_a, HD // 4, cka_o if emit_cache else None, 128)
    va = proj(_VA2)
    va_o[...] = va.astype(va_o.dtype)
    if emit_cache:
        cva_o[...] = va[:, 0:128]
    y, g = qk(_QB, g32_ref, gqb_ref, 384)
    finish(y, g, qb_o, tabs_b, B_DK // 4)
    y, g = qk(_KB, g32_ref, gkb_ref, 384)
    finish(y, g, kb_o, tabs_b, B_DK // 4, ckb_o if emit_cache else None, 384)
    vb = proj(_VB)
    vb_o[...] = vb.astype(vb_o.dtype)
    if emit_cache:
        cvb_o[...] = vb
    u_o[...] = proj(_CC) * proj(_CI)
    cbg_o[...] = proj(_CB)


def _pre_mix(x, mod_l, row_of_tile, n1g, win, gains, gmats, rope_tabs, *, tm, tiles_per_seq, emit_cache):
    t, d = x.shape
    nt = t // tm
    rope = rope_tabs is not None
    const = lambda shape: pl.BlockSpec(shape, lambda i: (0,) * len(shape))
    in_specs = [pl.BlockSpec((tm, d), lambda i: (i, 0)),
                pl.BlockSpec((None, 6, d), lambda i: (row_of_tile(i), 0, 0)),
                const((1, d)), const((d, NP_COLS)),
                const((1, 384)), const((1, 256)), const((1, 384)), const((1, 384)),
                const((384, 384)), const((384, 384))]
    args = [x, mod_l, n1g, win, *gains, *gmats]
    if rope:
        in_specs += [pl.BlockSpec((tm, LANES), lambda i: (i % tiles_per_seq, 0))] * 6
        args += list(rope_tabs)
    widths = [(384, BF16), (256, BF16), (256, BF16), (384, BF16), (384, BF16), (384, BF16),
              (256, F32), (256, F32)]
    if emit_cache:
        widths += [(128, F32), (128, F32), (384, F32), (384, F32)]
    out_shape = [jax.ShapeDtypeStruct((t, w), dt) for w, dt in widths]
    out_specs = [pl.BlockSpec((tm, w), lambda i: (i, 0)) for w, _ in widths]
    return pl.pallas_call(
        functools.partial(_pre_mix_kernel, rope=rope, emit_cache=emit_cache),
        grid=(nt,), in_specs=in_specs, out_specs=out_specs, out_shape=out_shape,
        compiler_params=_params(("parallel",)),
        name="pre_mix_rope" if rope else "pre_mix",
    )(*args)


def _softmax_pv(q, k_parts, v_parts):
    ss = [_dot_nt(q, k) for k in k_parts]
    m = jnp.max(ss[0], axis=-1, keepdims=True)
    for s in ss[1:]:
        m = jnp.maximum(m, jnp.max(s, axis=-1, keepdims=True))
    l = None
    o = None
    for s, v in zip(ss, v_parts):
        p = jnp.exp(s - m)
        ls = jnp.sum(p, axis=-1, keepdims=True)
        os_ = _dot(p.astype(BF16), v)
        l = ls if l is None else l + ls
        o = os_ if o is None else o + os_
    return o * (1.0 / l)


def _attn_kernel(*refs, has_cache, lam_init):
    it = iter(refs)
    qa_ref, ka_ref, va_ref, qb_ref, kb_ref, vb_ref = (next(it) for _ in range(6))
    if has_cache:
        cka_ref, cva_ref, ckb_ref, cvb_ref = (next(it) for _ in range(4))
    lq1_ref, lk1_ref, lq2_ref, lk2_ref = (next(it) for _ in range(4))
    oa_ref, ob_ref = next(it), next(it)

    tq = qa_ref.shape[0]
    lane = lax.broadcasted_iota(jnp.int32, (tq, LANES), 1)
    lo_half = lane < 64

    lam = (jnp.exp(jnp.sum(lq1_ref[...] * lk1_ref[...], axis=-1, keepdims=True))
           - jnp.exp(jnp.sum(lq2_ref[...] * lk2_ref[...], axis=-1, keepdims=True)) + lam_init)

    for m in range(A_HEADS // 2):
        qblk = qa_ref[:, m * LANES:(m + 1) * LANES]
        outs = []
        for par in range(2):
            kv = (2 * m + par) // (A_HEADS // A_KV)
            kblk = 0 if kv == par else 1
            ksl = slice(kblk * LANES, (kblk + 1) * LANES)
            qh = jnp.where(lo_half if par == 0 else jnp.logical_not(lo_half), qblk, jnp.zeros_like(qblk))
            ks, vs = [ka_ref[:, ksl]], [va_ref[:, ksl]]
            if has_cache:
                ks.append(cka_ref[:, ksl])
                vs.append(cva_ref[:, ksl])
            outs.append(_softmax_pv(qh, ks, vs))
        oa_ref[:, m * LANES:(m + 1) * LANES] = jnp.where(lo_half, outs[0], outs[1])

    seg_id = lane >> 5
    for m in range(B_HEADS // 2):
        sl = slice(m * LANES, (m + 1) * LANES)
        qblk = qb_ref[:, sl]
        ks, vs = [kb_ref[:, sl]], [vb_ref[:, sl]]
        if has_cache:
            ks.append(ckb_ref[:, sl])
            vs.append(cvb_ref[:, sl])
        outs = []
        for par in range(2):
            maps = []
            for j in range(2):
                qs = jnp.where(seg_id == (2 * par + j), qblk, jnp.zeros_like(qblk))
                maps.append(_softmax_pv(qs, ks, vs))
            outs.append(maps[0] - lam * maps[1])
        ob_ref[:, sl] = jnp.where(lo_half, outs[0], outs[1])


def _attention(qa, ka2, va2, qb, kb, vb, caches, lams, *, layer, lam_init, tq):
    b, s, _ = qa.shape
    has_cache = caches is not None
    qspec = lambda w: pl.BlockSpec((None, tq, w), lambda i, j: (i, j, 0))
    kspec = lambda w: pl.BlockSpec((None, s, w), lambda i, j: (i, 0, 0))
    in_specs = [qspec(384), kspec(256), kspec(256), qspec(384), kspec(384), kspec(384)]
    args = [qa, ka2, va2, qb, kb, vb]
    if has_cache:
        lc = caches[0].shape[2]
        cspec = lambda w: pl.BlockSpec((None, None, lc, w), lambda i, j: (i, layer, 0, 0))
        in_specs += [cspec(256), cspec(256), cspec(384), cspec(384)]
        args += list(caches)
    in_specs += [pl.BlockSpec((1, B_DK), lambda i, j: (0, 0))] * 4
    args += list(lams)
    return pl.pallas_call(
        functools.partial(_attn_kernel, has_cache=has_cache, lam_init=lam_init),
        grid=(b, s // tq), in_specs=in_specs,
        out_specs=[qspec(384), qspec(384)],
        out_shape=[jax.ShapeDtypeStruct((b, s, 384), F32)] * 2,
        compiler_params=_params(("parallel", "parallel")),
        name="attention_cached" if has_cache else "attention",
    )(*args)


def _post_mix_kernel(oa_ref, ob_ref, u_ref, up_ref, un_ref, cbg_ref, x_ref, mod_ref,
                     ga_ref, gb_ref, g64_ref, gc_ref, cw_ref, cbias_ref, wout_ref, o_ref,
                     *, tiles_per_seq):
    i = pl.program_id(0)
    tm = x_ref.shape[0]
    pos = i % tiles_per_seq
    u = u_ref[...]
    row = lax.broadcasted_iota(jnp.int32, u.shape, 0)
    prev_row = jnp.where(pos == 0, 0.0, 1.0) * up_ref[SUBLANES - 1:SUBLANES, :]
    next_row = jnp.where(pos == tiles_per_seq - 1, 0.0, 1.0) * un_ref[0:1, :]
    u_dn = jnp.where(row == 0, prev_row, pltpu.roll(u, 1, 0))
    u_up = jnp.where(row == tm - 1, next_row, pltpu.roll(u, tm - 1, 0))
    y = u_dn * cw_ref[0:1, :] + u * cw_ref[1:2, :] + u_up * cw_ref[2:3, :] + cbias_ref[...]
    oc = cbg_ref[...] * y

    na = _rms(oa_ref[...]) * ga_ref[...]
    ob = ob_ref[...]
    nb = ob * lax.rsqrt(_group_mean_sq(ob, g64_ref) + EPS) * gb_ref[...]
    nc = _rms(oc) * gc_ref[...]
    mix = (_dot(na.astype(BF16), wout_ref[0:384, :]) + _dot(nb.astype(BF16), wout_ref[384:768, :])
           + _dot(nc.astype(BF16), wout_ref[768:1024, :]))
    o_ref[...] = x_ref[...] + mod_ref[2:3, :] * mix


def _post_mix(oa, ob, u, cbg, x, mod_l, row_of_tile, ga, gb, g64, gc, cw, cbias, wout, *, tm, tiles_per_seq):
    t, d = x.shape
    nt = t // tm
    nb8 = t // SUBLANES
    step = tm // SUBLANES
    const = lambda shape: pl.BlockSpec(shape, lambda i: (0,) * len(shape))
    tok = lambda w: pl.BlockSpec((tm, w), lambda i: (i, 0))
    in_specs = [tok(384), tok(384), tok(256),
                pl.BlockSpec((SUBLANES, 256), lambda i: (jnp.maximum(i * step - 1, 0), 0)),
                pl.BlockSpec((SUBLANES, 256), lambda i: (jnp.minimum((i + 1) * step, nb8 - 1), 0)),
                tok(256), tok(d),
                pl.BlockSpec((None, 6, d), lambda i: (row_of_tile(i), 0, 0)),
                const((1, 384)), const((1, 384)), const((384, 384)), const((1, 256)),
                const((3, 256)), const((1, 256)), const((d, d))]
    return pl.pallas_call(
        functools.partial(_post_mix_kernel, tiles_per_seq=tiles_per_seq),
        grid=(nt,), in_specs=in_specs, out_specs=tok(d),
        out_shape=jax.ShapeDtypeStruct((t, d), F32),
        compiler_params=_params(("parallel",)),
        name="post_mix",
    )(oa, ob, u, u, u, cbg, x, mod_l, ga, gb, g64, gc, cw, cbias, wout)


def _gelu(x):
    k1 = -2.0 * math.sqrt(2.0 / math.pi) * math.log2(math.e)
    return x / (1.0 + jnp.exp2(x * (k1 + (k1 * 0.044715) * (x * x))))


BF16_SUBLANES = 16


def _bcast_row_bf16(row):
    packed = jnp.broadcast_to(row, (BF16_SUBLANES, LANES)).astype(BF16)
    return jnp.concatenate([packed] * (N_KEYS // BF16_SUBLANES), axis=0)


def _top16(s):
    n = s.shape[0]
    rows = lax.broadcasted_iota(jnp.int32, s.shape, 0).astype(F32)
    work = s
    rank = jnp.full(s.shape, float(P_TOPK), F32)
    vals = []
    for a in range(P_TOPK):
        m = jnp.max(work, axis=0, keepdims=True)
        idx = jnp.min(jnp.where(work == m, rows, float(n)), axis=0, keepdims=True)
        hit = rows == idx
        rank = jnp.where(hit, float(a), rank)
        work = jnp.where(hit, NEG_INF, work)
        vals.append(m)
    return vals, rank


def _top16_distinct(s, want_rank):
    work = s
    rank = jnp.full(s.shape, float(P_TOPK), F32) if want_rank else None
    vals = []
    for a in range(P_TOPK):
        m = jnp.max(work, axis=0, keepdims=True)
        hit = work == m
        if want_rank:
            rank = jnp.where(hit, float(a), rank)
        work = jnp.where(hit, NEG_INF, work)
        vals.append(m)
    return vals, rank


def _stack16(rows_list):
    w = rows_list[0].shape[1]
    r = lax.broadcasted_iota(jnp.int32, (P_TOPK, w), 0)
    out = jnp.zeros((P_TOPK, w), F32)
    for a, v in enumerate(rows_list):
        out = jnp.where(r == a, v, out)
    return out


_CAND_ROWS = 80


def _cand_build(v1_rows, v1_arr, v2_arr, v2_row0, op):
    pieces = [op(v1_rows[0], v2_arr), op(v1_rows[1], v2_arr[0:8])]
    for a in range(2, 8):
        pieces.append(op(v1_rows[a], v2_arr[0:8]))
    pieces.append(op(v1_arr[8:16], v2_row0))
    return jnp.concatenate(pieces, axis=0)


def _cand_index(w):
    row = lax.broadcasted_iota(jnp.int32, (_CAND_ROWS, w), 0)
    a = jnp.where(row < 16, 0, jnp.where(row < 72, 1 + ((row - 16) >> 3), 8 + (row - 72)))
    b = jnp.where(row < 16, row, jnp.where(row < 72, (row - 16) & 7, 0))
    valid = (a + 1) * (b + 1) <= P_TOPK
    flat = (a * P_TOPK + b).astype(F32)
    return valid, flat


def _peer_head_tables(s1, s2, valid, flat, exact):
    if exact:
        v1_rows, rank1 = _top16(s1)
        v2_rows, rank2 = _top16(s2)
    else:
        v1_rows, _ = _top16_distinct(s1, False)
        v2_rows, rank2 = _top16_distinct(s2, True)
    v1 = _stack16(v1_rows)
    v2 = _stack16(v2_rows)
    e1 = jnp.exp(v1 - v1_rows[0])
    e2 = jnp.exp(v2 - v2_rows[0])
    e1_rows = [e1[a:a + 1, :] for a in range(8)]
    cand = _cand_build(v1_rows, v1, v2, v2_rows[0], lambda p, q: p + q)
    gcand = _cand_build(e1_rows, e1, e2, e2[0:1, :], lambda p, q: p * q)
    work = jnp.where(valid, cand, NEG_INF)
    sel = jnp.zeros_like(cand)
    for _ in range(P_TOPK):
        m = jnp.max(work, axis=0, keepdims=True)
        idx = jnp.min(jnp.where(work == m, flat, 1e9), axis=0, keepdims=True)
        hit = flat == idx
        sel = jnp.where(hit, 1.0, sel)
        work = jnp.where(hit, NEG_INF, work)
    z = jnp.sum(sel * gcand, axis=0, keepdims=True)
    counts = [jnp.sum(sel[0:16], axis=0, keepdims=True)]
    for a in range(1, 8):
        counts.append(jnp.sum(sel[8 + 8 * a:16 + 8 * a], axis=0, keepdims=True))
    for a in range(8, 16):
        counts.append(sel[64 + a:65 + a, :])
    ltab = jnp.zeros_like(s1)
    if exact:
        member = rank1 < float(P_TOPK)
        for a in range(P_TOPK):
            ltab = jnp.where(rank1 == float(a), counts[a], ltab)
        bad = None
    else:
        member = s1 >= v1_rows[P_TOPK - 1]
        for a in range(P_TOPK):
            ltab = jnp.where(s1 == v1_rows[a], counts[a], ltab)
        n1 = jnp.sum(jnp.where(member, 1.0, 0.0), axis=0, keepdims=True)
        n2 = jnp.sum(jnp.where(rank2 < float(P_TOPK), 1.0, 0.0), axis=0, keepdims=True)
        bad = jnp.where(n1 + n2 != 2.0 * P_TOPK, 1.0, 0.0)
    atab = jnp.where(member, jnp.exp(s1 - v1_rows[0]) * (1.0 / z), 0.0)
    btab = jnp.exp(s2 - v2_rows[0])
    return ltab, atab, rank2, btab, bad


def _peer_prologue(x_ref, mod_ref, n2g_ref, wqt_ref, keys_ref, ht_scr, qt_scr,
                   rank2_scr, b_scr, l_scr, a_scr, acc_scr, bad_scr):
    tm = x_ref.shape[0]
    ncb = tm // LANES
    nq = wqt_ref.shape[0]
    x = x_ref[...]
    h = _rms(x) * n2g_ref[...] * (1.0 + mod_ref[4:5, :]) + mod_ref[3:4, :]
    htb = h.T.astype(BF16)
    ht_scr[...] = htb
    for half in range(2):
        rs = slice(half * (nq // 2), (half + 1) * (nq // 2))
        qt = _dot(wqt_ref[rs, :], htb)
        for cb in range(ncb):
            qt_scr[cb, rs, :] = qt[:, cb * LANES:(cb + 1) * LANES]
    acc_scr[...] = jnp.zeros_like(acc_scr)

    valid, flat = _cand_index(LANES)

    def scores(hd, cb):
        r1 = pl.multiple_of(hd * (2 * P_HALF), 2 * P_HALF)
        q1 = qt_scr[cb, pl.ds(r1, P_HALF), :].astype(BF16)
        q2 = qt_scr[cb, pl.ds(r1 + P_HALF, P_HALF), :].astype(BF16)
        return _dot(keys_ref[2 * hd], q1), _dot(keys_ref[2 * hd + 1], q2)

    def store(hd, cb, ltab, atab, rank2, btab):
        n = hd * ncb + cb
        l_scr[n] = ltab
        a_scr[n] = atab
        rank2_scr[n] = rank2.astype(rank2_scr.dtype)
        b_scr[n] = btab.astype(b_scr.dtype)

    def head_body(hd, carry):
        bad = jnp.zeros((1, LANES), F32)
        for cb in range(ncb):
            s1, s2 = scores(hd, cb)
            ltab, atab, rank2, btab, bad_cb = _peer_head_tables(s1, s2, valid, flat, exact=False)
            store(hd, cb, ltab, atab, rank2, btab)
            bad = jnp.maximum(bad, bad_cb)
        bad_scr[hd] = jnp.broadcast_to(bad, (SUBLANES, LANES))
        return carry

    lax.fori_loop(0, P_HEADS, head_body, 0)

    def redo_body(hd, carry):
        @pl.when(jnp.max(bad_scr[hd]) > 0.0)
        def _():
            for cb in range(ncb):
                s1, s2 = scores(hd, cb)
                ltab, atab, rank2, btab, _ = _peer_head_tables(s1, s2, valid, flat, exact=True)
                store(hd, cb, ltab, atab, rank2, btab)
        return carry

    lax.fori_loop(0, P_HEADS, redo_body, 0)


PEER_SUB = 512
MXU_COLS = 256
PEER_VMEM_LIMIT = 56 * 1024 * 1024


def _peer_kernel(x_ref, mod_ref, n2g_ref, wqt_ref, keys_ref, u_ref, vt_ref, o_ref,
                 ht_scr, qt_scr, rank2_scr, b_scr, l_scr, a_scr, acc_scr, bad_scr):
    j = pl.program_id(1)
    nj = pl.num_programs(1)
    tm = x_ref.shape[0]
    ncb = tm // LANES
    ec = u_ref.shape[0]
    rows_per_sub = PEER_SUB // N_KEYS

    @pl.when(j == 0)
    def _():
        _peer_prologue(x_ref, mod_ref, n2g_ref, wqt_ref, keys_ref, ht_scr, qt_scr,
                       rank2_scr, b_scr, l_scr, a_scr, acc_scr, bad_scr)

    n_half = 2
    eh = ec // n_half
    for half in range(n_half):
        for nb in range(tm // MXU_COLS):
            ns = slice(nb * MXU_COLS, (nb + 1) * MXU_COLS)
            act = _dot(u_ref[half * eh:(half + 1) * eh, :], ht_scr[:, ns])
            g_rows = []
            for q in range(eh // PEER_SUB):
                cols = []
                for c2 in range(MXU_COLS // LANES):
                    cb = nb * (MXU_COLS // LANES) + c2
                    ws = [jnp.zeros((N_KEYS, LANES), BF16) for _ in range(rows_per_sub)]
                    for hd in range(P_HEADS):
                        n = hd * ncb + cb
                        rank2 = rank2_scr[n]
                        btab = b_scr[n]
                        for r in range(rows_per_sub):
                            i1 = j * (ec // N_KEYS) + (half * eh + q * PEER_SUB) // N_KEYS + r
                            lrow = _bcast_row_bf16(l_scr[n, pl.ds(i1, 1), :])
                            arow = _bcast_row_bf16(a_scr[n, pl.ds(i1, 1), :])
                            ws[r] = ws[r] + jnp.where(rank2 < lrow, btab * arow, jnp.zeros_like(btab))
                    gs = []
                    for r in range(rows_per_sub):
                        r0 = q * PEER_SUB + r * N_KEYS
                        a_blk = act[r0:r0 + N_KEYS, c2 * LANES:(c2 + 1) * LANES]
                        gs.append(ws[r] * _gelu(a_blk.astype(BF16)))
                    cols.append(jnp.concatenate(gs, axis=0))
                g_rows.append(jnp.concatenate(cols, axis=1))
            g = jnp.concatenate(g_rows, axis=0)
            acc_scr[:, ns] += _dot(vt_ref[:, half * eh:(half + 1) * eh], g)

    @pl.when(j == nj - 1)
    def _():
        o_ref[...] = x_ref[...] + mod_ref[5:6, :] * acc_scr[...].T


def _peer(x, mod_l, row_of_tile, n2g, wqt, keys, u_tab, vt_tab, *, tm, ec):
    t, d = x.shape
    nt = t // tm
    ne = u_tab.shape[0]
    ncb = tm // LANES
    nq = wqt.shape[0]
    tabs = lambda dt: pltpu.VMEM((P_HEADS * ncb, N_KEYS, LANES), dt)
    return pl.pallas_call(
        _peer_kernel,
        grid=(nt, ne // ec),
        in_specs=[pl.BlockSpec((tm, d), lambda i, j: (i, 0)),
                  pl.BlockSpec((None, 6, d), lambda i, j: (row_of_tile(i), 0, 0)),
                  pl.BlockSpec((1, d), lambda i, j: (0, 0)),
                  pl.BlockSpec((nq, d), lambda i, j: (0, 0), pipeline_mode=pl.Buffered(1)),
                  pl.BlockSpec((2 * P_HEADS, N_KEYS, P_HALF), lambda i, j: (0, 0, 0),
                               pipeline_mode=pl.Buffered(1)),
                  pl.BlockSpec((ec, d), lambda i, j: (j, 0)),
                  pl.BlockSpec((d, ec), lambda i, j: (0, j))],
        out_specs=pl.BlockSpec((tm, d), lambda i, j: (i, 0)),
        out_shape=jax.ShapeDtypeStruct((t, d), F32),
        scratch_shapes=[pltpu.VMEM((d, tm), BF16),
                        pltpu.VMEM((ncb, nq, LANES), F32),
                        tabs(BF16), tabs(BF16), tabs(F32), tabs(F32),
                        pltpu.VMEM((d, tm), F32),
                        pltpu.VMEM((P_HEADS, SUBLANES, LANES), F32)],
        compiler_params=_params(("parallel", "arbitrary"), PEER_VMEM_LIMIT),
        name="peer",
    )(x, mod_l, n2g, wqt, keys, u_tab, vt_tab)


def _rope_tables(n_tokens, dim):
    quarter = dim // 4
    pos = np.arange(n_tokens)
    row = (pos // GRID_W).astype(np.float32)
    col = (pos % GRID_W).astype(np.float32)
    lane = np.arange(LANES)
    dd = lane % dim
    half = dd // (dim // 2)
    e = dd % (dim // 2)
    second = e // quarter
    freq = e % quarter
    inv = jnp.asarray(ROPE_BASE, F32) ** (-jnp.arange(quarter, dtype=F32) / quarter)
    inv_lane = inv[freq]
    p = jnp.where(jnp.asarray(half == 0)[None, :], jnp.asarray(row)[:, None], jnp.asarray(col)[:, None])
    ang = p * inv_lane[None, :]
    cos, sin = jnp.cos(ang), jnp.sin(ang)
    first = jnp.asarray(second == 0)[None, :]
    return cos, jnp.where(first, -sin, 0.0), jnp.where(first, 0.0, sin)


def _block_diag_mean(width, group):
    idx = np.arange(width) // group
    return jnp.asarray((idx[:, None] == idx[None, :]).astype(np.float32) / group, BF16)


def kernel(x_prompt, x_sample, cache_a_k, cache_a_v, cache_b_k, cache_b_v, c, c_ctx, ada_w, ada_b, norm1_g, norm2_g, w_in, qn_a, kn_a, qn_b, kn_b, lam_q1, lam_k1, lam_q2, lam_k2, subln_g, out_norm_a, out_norm_c, conv_w, conv_b, w_out, peer_wq, peer_keys, peer_u, peer_v):
    nb, s_p, d = x_prompt.shape
    db, s_l, _ = x_sample.shape
    depth = w_in.shape[0]
    past = cache_a_k.shape[2]
    tm_p = s_p
    tm_l = min(MIX_TILE, s_l)
    tq_l = min(256, s_l)
    tm_pre_p = min(MIX_TILE, nb * s_p)
    tm_peer = TOKEN_TILE
    ec = 2048
    assert s_l % tm_l == 0 and s_l % GRID_W == 0 and d == 1024 and (nb * s_p) % tm_pre_p == 0
    assert s_l % tm_peer == 0 and (nb * s_p) % tm_peer == 0 and tm_p % SUBLANES == 0

    n_cond = 1 + db
    n_rows = -(-n_cond // SUBLANES) * SUBLANES
    cond = jnp.concatenate([c_ctx[None, :], c, jnp.zeros((n_rows - n_cond, d), F32)], axis=0)
    mod = _modulation(cond, ada_w, ada_b).reshape(depth, n_rows, 6, d)

    def split_cols(w):
        offs = np.cumsum([0, 384, 128, 128, 384, 384, 384, 256, 256, 256])
        return [w[..., offs[k]:offs[k + 1]] for k in range(9)]

    def swap_kv(w):
        return jnp.concatenate([w[..., 64:128], w[..., 0:64]], axis=-1)

    wqa, wka, wva, wqb, wkb, wvb, wci, wcb, wcc = split_cols(w_in)
    win = jnp.concatenate([wqa, wka, swap_kv(wka), wva, swap_kv(wva), wqb, wkb, wvb, wci, wcb, wcc],
                          axis=-1).astype(BF16)
    wout = w_out.astype(BF16)
    wqt = jnp.swapaxes(peer_wq, 1, 2).astype(BF16)
    keys = peer_keys.reshape(depth, 2 * P_HEADS, N_KEYS, P_HALF).astype(BF16)
    u_tab = peer_u.astype(BF16)
    vt_tab = jnp.swapaxes(peer_v, 1, 2).astype(BF16)

    g64 = _block_diag_mean(384, HD)
    g32 = _block_diag_mean(384, B_DK)
    rope_a = _rope_tables(s_l, HD)
    rope_b = _rope_tables(s_l, B_DK)

    ck = cache_a_k.reshape(db, depth, past, 128)
    cv = cache_a_v.reshape(db, depth, past, 128)
    caches = (jnp.concatenate([ck, swap_kv(ck)], axis=-1).astype(BF16),
              jnp.concatenate([cv, swap_kv(cv)], axis=-1).astype(BF16),
              cache_b_k.reshape(db, depth, past, 384).astype(BF16),
              cache_b_v.reshape(db, depth, past, 384).astype(BF16))

    xp = x_prompt.reshape(nb * s_p, d)
    xs = x_sample.reshape(db * s_l, d)
    tps_p = s_p // tm_p
    tps_l = s_l // tm_l
    row_p = lambda i: 0
    row_l = lambda i: 1 + i // tps_l
    row_l_peer = lambda i: 1 + i // (s_l // tm_peer)

    new = [[], [], [], []]
    for l in range(depth):
        lam_init = 0.8 - 0.6 * math.exp(-0.3 * l)
        gains = ((jnp.tile(qn_a[l], A_HEADS) * HD ** -0.5)[None, :],
                 jnp.tile(kn_a[l], 2 * A_KV)[None, :],
                 (jnp.tile(qn_b[l], 2 * B_HEADS) * B_DK ** -0.5)[None, :],
                 jnp.tile(kn_b[l], 2 * B_HEADS)[None, :])
        lams = (lam_q1[l][None, :], lam_k1[l][None, :], lam_q2[l][None, :], lam_k2[l][None, :])
        gb = (jnp.tile(subln_g[l], B_HEADS) * (1.0 - lam_init))[None, :]
        post_args = (out_norm_a[l][None, :], gb, g64, out_norm_c[l][None, :], conv_w[l], conv_b[l][None, :],
                     wout[l])
        n1g = norm1_g[l][None, :]
        n2g = norm2_g[l][None, :]

        qa, ka2, va2, qb, kb, vb, u, cbg, cka, cva, ckb, cvb = _pre_mix(
            xp, mod[l], row_p, n1g, win[l], gains, (g64, g32), None,
            tm=tm_pre_p, tiles_per_seq=1, emit_cache=True)
        new[0].append(cka.reshape(nb, s_p, A_KV, HD))
        new[1].append(cva.reshape(nb, s_p, A_KV, HD))
        new[2].append(ckb.reshape(nb, s_p, B_HEADS, 2, B_DK))
        new[3].append(cvb.reshape(nb, s_p, B_HEADS, 2 * B_DK))
        r3 = lambda a: a.reshape(nb, s_p, a.shape[-1])
        oa, ob = _attention(r3(qa), r3(ka2), r3(va2), r3(qb), r3(kb), r3(vb), None, lams,
                            layer=l, lam_init=lam_init, tq=s_p)
        xp = _post_mix(oa.reshape(nb * s_p, 384), ob.reshape(nb * s_p, 384), u, cbg, xp, mod[l], row_p,
                       *post_args, tm=tm_p, tiles_per_seq=tps_p)
        xp = _peer(xp, mod[l], row_p, n2g, wqt[l], keys[l], u_tab[l], vt_tab[l], tm=tm_peer, ec=ec)

        qa, ka2, va2, qb, kb, vb, u, cbg = _pre_mix(
            xs, mod[l], row_l, n1g, win[l], gains, (g64, g32), rope_a + rope_b,
            tm=tm_l, tiles_per_seq=tps_l, emit_cache=False)
        r3 = lambda a: a.reshape(db, s_l, a.shape[-1])
        oa, ob = _attention(r3(qa), r3(ka2), r3(va2), r3(qb), r3(kb), r3(vb), caches, lams,
                            layer=l, lam_init=lam_init, tq=tq_l)
        xs = _post_mix(oa.reshape(db * s_l, 384), ob.reshape(db * s_l, 384), u, cbg, xs, mod[l], row_l,
                       *post_args, tm=tm_l, tiles_per_seq=tps_l)
        xs = _peer(xs, mod[l], row_l_peer, n2g, wqt[l], keys[l], u_tab[l], vt_tab[l], tm=tm_peer, ec=ec)

    return (xp.reshape(nb, s_p, d), xs.reshape(db, s_l, d),
            jnp.stack(new[0], axis=1), jnp.stack(new[1], axis=1),
            jnp.stack(new[2], axis=1), jnp.stack(new[3], axis=1))
```

```python
import functools
import math

import numpy as np
import jax
import jax.numpy as jnp
from jax import lax
from jax.experimental import pallas as pl
from jax.experimental.pallas import tpu as pltpu

F32 = jnp.float32
BF16 = jnp.bfloat16

HD = 64
A_HEADS = 6
A_KV = 2
B_HEADS = 6
B_DK = 32
C_WIDTH = 256
A_WIDTH = A_HEADS * HD
B_WIDTH = B_HEADS * 2 * B_DK
GRID_W = 64
ROPE_BASE = 10000.0
EPS = 1e-6
P_HEADS = 8
N_KEYS = 128
P_HALF = 128
P_TOPK = 16

LANES = 128
SUBLANES = 8
VMEM_LIMIT = 48 * 1024 * 1024
TOKEN_TILE = 512
MIX_TILE = 1024

_QA = (0, 384)
_KA2 = (384, 640)
_VA2 = (640, 896)
_QB = (896, 1280)
_KB = (1280, 1664)
_VB = (1664, 2048)
_CI = (2048, 2304)
_CB = (2304, 2560)
_CC = (2560, 2816)
NP_COLS = 2816

NEG_INF = float("-inf")


def _params(sem, vmem=VMEM_LIMIT):
    return pltpu.CompilerParams(dimension_semantics=sem, vmem_limit_bytes=vmem)


def _dot(a, b):
    return jnp.dot(a, b, preferred_element_type=F32)


def _dot_nt(a, b):
    return lax.dot_general(a, b, (((1,), (1,)), ((), ())), preferred_element_type=F32)


def _split3(a):
    hi = a.astype(BF16)
    r = a - hi.astype(F32)
    mid = r.astype(BF16)
    lo = (r - mid.astype(F32)).astype(BF16)
    return hi, mid, lo


def _dot_hp(a, b):
    a0, a1, a2 = _split3(a)
    b0, b1, b2 = _split3(b)
    return (_dot(a0, b0) + (_dot(a0, b1) + _dot(a1, b0))
            + (_dot(a0, b2) + _dot(a1, b1) + _dot(a2, b0)))


def _group_mean_sq(v, g_ref):
    v2 = v * v
    hi = v2.astype(BF16)
    lo = (v2 - hi.astype(F32)).astype(BF16)
    g = g_ref[...]
    return _dot(hi, g) + _dot(lo, g)


def _rms(x):
    return x * lax.rsqrt(jnp.mean(x * x, axis=-1, keepdims=True) + EPS)


def _mod_kernel(cond_ref, w_ref, b_ref, o_ref):
    c = cond_ref[...]
    s = c / (1.0 + jnp.exp(-c))
    o_ref[...] = _dot_hp(s, w_ref[...]) + b_ref[...]


def _modulation(cond, ada_w, ada_b):
    depth, d, d6 = ada_w.shape
    n = cond.shape[0]
    tn = 1536
    return pl.pallas_call(
        _mod_kernel,
        grid=(depth, d6 // tn),
        in_specs=[pl.BlockSpec((n, d), lambda l, j: (0, 0)),
                  pl.BlockSpec((None, d, tn), lambda l, j: (l, 0, j)),
                  pl.BlockSpec((None, 1, tn), lambda l, j: (l, 0, j))],
        out_specs=pl.BlockSpec((None, n, tn), lambda l, j: (l, 0, j)),
        out_shape=jax.ShapeDtypeStruct((depth, n, d6), F32),
        compiler_params=_params(("parallel", "parallel")),
        name="modulation",
    )(cond, ada_w, ada_b.reshape(depth, 1, d6))


def _rope_block(blk, c, sm, sp, shift):
    return (blk * c + pltpu.roll(blk, LANES - shift, 1) * sm + pltpu.roll(blk, shift, 1) * sp)


def _pre_mix_kernel(*refs, rope, emit_cache):
    it = iter(refs)
    x_ref, mod_ref, n1g_ref, win_ref = next(it), next(it), next(it), next(it)
    gqa_ref, gka_ref, gqb_ref, gkb_ref = next(it), next(it), next(it), next(it)
    g64_ref, g32_ref = next(it), next(it)
    if rope:
        ca_ref, sma_ref, spa_ref = next(it), next(it), next(it)
        cb_ref, smb_ref, spb_ref = next(it), next(it), next(it)
    qa_o, ka_o, va_o, qb_o, kb_o, vb_o, u_o, cbg_o = (next(it) for _ in range(8))
    if emit_cache:
        cka_o, cva_o, ckb_o, cvb_o = (next(it) for _ in range(4))

    x = x_ref[...]
    sh1 = mod_ref[0:1, :]
    sc1 = mod_ref[1:2, :]
    h = _rms(x) * n1g_ref[...] * (1.0 + sc1) + sh1
    hb = h.astype(BF16)

    def proj(seg):
        return _dot(hb, win_ref[:, seg[0]:seg[1]])

    def qk(seg, g_ref, gain_ref, width):
        p = proj(seg)
        ms = _group_mean_sq(p, g_ref) if width == 384 else _dot_group256(p, g_ref)
        return p * lax.rsqrt(ms + EPS), gain_ref[...]

    def _dot_group256(p, g_ref):
        v2 = p * p
        hi = v2.astype(BF16)
        lo = (v2 - hi.astype(F32)).astype(BF16)
        g = g_ref[0:256, 0:256]
        return _dot(hi, g) + _dot(lo, g)

    def finish(y, gain, out_ref, tabs, shift, cache_ref=None, cache_cols=None):
        nblk = y.shape[1] // LANES
        for m in range(nblk):
            sl = slice(m * LANES, (m + 1) * LANES)
            blk = y[:, sl] * gain[:, sl]
            if cache_ref is not None and m < cache_cols // LANES:
                cache_ref[:, sl] = blk
            if tabs is not None:
                blk = _rope_block(blk, tabs[0][...], tabs[1][...], tabs[2][...], shift)
            out_ref[:, sl] = blk.astype(out_ref.dtype)

    tabs_a = (ca_ref, sma_ref, spa_ref) if rope else None
    tabs_b = (cb_ref, smb_ref, spb_ref) if rope else None

    y, g = qk(_QA, g64_ref, gqa_ref, 384)
    finish(y, g, qa_o, tabs_a, HD // 4)
    y, g = qk(_KA2, g64_ref, gka_ref, 256)
    finish(y, g, ka_o, tabs_a, HD // 4, cka_o if emit_cache else None, 128)
    va = proj(_VA2)
    va_o[...] = va.astype(va_o.dtype)
    if emit_cache:
        cva_o[...] = va[:, 0:128]
    y, g = qk(_QB, g32_ref, gqb_ref, 384)
    finish(y, g, qb_o, tabs_b, B_DK // 4)
    y, g = qk(_KB, g32_ref, gkb_ref, 384)
    finish(y, g, kb_o, tabs_b, B_DK // 4, ckb_o if emit_cache else None, 384)
    vb = proj(_VB)
    vb_o[...] = vb.astype(vb_o.dtype)
    if emit_cache:
        cvb_o[...] = vb
    u_o[...] = proj(_CC) * proj(_CI)
    cbg_o[...] = proj(_CB)


def _pre_mix(x, mod_l, row_of_tile, n1g, win, gains, gmats, rope_tabs, *, layer, tm, tiles_per_seq, emit_cache):
    t, d = x.shape
    nt = t // tm
    rope = rope_tabs is not None
    const = lambda shape: pl.BlockSpec(shape, lambda i: (0,) * len(shape))
    in_specs = [pl.BlockSpec((tm, d), lambda i: (i, 0)),
                pl.BlockSpec((None, 6, d), lambda i: (row_of_tile(i), 0, 0)),
                const((1, d)), pl.BlockSpec((None, d, NP_COLS), lambda i: (layer, 0, 0)),
                const((1, 384)), const((1, 256)), const((1, 384)), const((1, 384)),
                const((384, 384)), const((384, 384))]
    args = [x, mod_l, n1g, win, *gains, *gmats]
    if rope:
        in_specs += [pl.BlockSpec((tm, LANES), lambda i: (i % tiles_per_seq, 0))] * 6
        args += list(rope_tabs)
    widths = [(384, BF16), (256, BF16), (256, BF16), (384, BF16), (384, BF16), (384, BF16),
              (256, F32), (256, F32)]
    if emit_cache:
        widths += [(128, F32), (128, F32), (384, F32), (384, F32)]
    out_shape = [jax.ShapeDtypeStruct((t, w), dt) for w, dt in widths]
    out_specs = [pl.BlockSpec((tm, w), lambda i: (i, 0)) for w, _ in widths]
    return pl.pallas_call(
        functools.partial(_pre_mix_kernel, rope=rope, emit_cache=emit_cache),
        grid=(nt,), in_specs=in_specs, out_specs=out_specs, out_shape=out_shape,
        compiler_params=_params(("parallel",)),
        name="pre_mix_rope" if rope else "pre_mix",
    )(*args)


def _softmax_pv(q, k_parts, v_parts):
    ss = [_dot_nt(q, k) for k in k_parts]
    m = jnp.max(ss[0], axis=-1, keepdims=True)
    for s in ss[1:]:
        m = jnp.maximum(m, jnp.max(s, axis=-1, keepdims=True))
    l = None
    o = None
    for s, v in zip(ss, v_parts):
        p = jnp.exp(s - m)
        ls = jnp.sum(p, axis=-1, keepdims=True)
        os_ = _dot(p.astype(BF16), v)
        l = ls if l is None else l + ls
        o = os_ if o is None else o + os_
    return o * (1.0 / l)


def _attn_kernel(*refs, has_cache, lam_init):
    it = iter(refs)
    qa_ref, ka_ref, va_ref, qb_ref, kb_ref, vb_ref = (next(it) for _ in range(6))
    if has_cache:
        cka_ref, cva_ref, ckb_ref, cvb_ref = (next(it) for _ in range(4))
    lq1_ref, lk1_ref, lq2_ref, lk2_ref = (next(it) for _ in range(4))
    oa_ref, ob_ref = next(it), next(it)

    tq = qa_ref.shape[0]
    lane = lax.broadcasted_iota(jnp.int32, (tq, LANES), 1)
    lo_half = lane < 64

    lam = (jnp.exp(jnp.sum(lq1_ref[...] * lk1_ref[...], axis=-1, keepdims=True))
           - jnp.exp(jnp.sum(lq2_ref[...] * lk2_ref[...], axis=-1, keepdims=True)) + lam_init)

    for m in range(A_HEADS // 2):
        qblk = qa_ref[:, m * LANES:(m + 1) * LANES]
        outs = []
        for par in range(2):
            kv = (2 * m + par) // (A_HEADS // A_KV)
            kblk = 0 if kv == par else 1
            ksl = slice(kblk * LANES, (kblk + 1) * LANES)
            qh = jnp.where(lo_half if par == 0 else jnp.logical_not(lo_half), qblk, jnp.zeros_like(qblk))
            ks, vs = [ka_ref[:, ksl]], [va_ref[:, ksl]]
            if has_cache:
                ks.append(cka_ref[:, ksl])
                vs.append(cva_ref[:, ksl])
            outs.append(_softmax_pv(qh, ks, vs))
        oa_ref[:, m * LANES:(m + 1) * LANES] = jnp.where(lo_half, outs[0], outs[1])

    seg_id = lane >> 5
    for m in range(B_HEADS // 2):
        sl = slice(m * LANES, (m + 1) * LANES)
        qblk = qb_ref[:, sl]
        ks, vs = [kb_ref[:, sl]], [vb_ref[:, sl]]
        if has_cache:
            ks.append(ckb_ref[:, sl])
            vs.append(cvb_ref[:, sl])
        outs = []
        for par in range(2):
            maps = []
            for j in range(2):
                qs = jnp.where(seg_id == (2 * par + j), qblk, jnp.zeros_like(qblk))
                maps.append(_softmax_pv(qs, ks, vs))
            outs.append(maps[0] - lam * maps[1])
        ob_ref[:, sl] = jnp.where(lo_half, outs[0], outs[1])


def _attention(qa, ka2, va2, qb, kb, vb, caches, lams, *, layer, lam_init, tq):
    b, s, _ = qa.shape
    has_cache = caches is not None
    qspec = lambda w: pl.BlockSpec((None, tq, w), lambda i, j: (i, j, 0))
    kspec = lambda w: pl.BlockSpec((None, s, w), lambda i, j: (i, 0, 0))
    in_specs = [qspec(384), kspec(256), kspec(256), qspec(384), kspec(384), kspec(384)]
    args = [qa, ka2, va2, qb, kb, vb]
    if has_cache:
        lc = caches[0].shape[2]
        cspec = lambda w: pl.BlockSpec((None, None, lc, w), lambda i, j: (i, layer, 0, 0))
        in_specs += [cspec(256), cspec(256), cspec(384), cspec(384)]
        args += list(caches)
    in_specs += [pl.BlockSpec((1, B_DK), lambda i, j: (0, 0))] * 4
    args += list(lams)
    return pl.pallas_call(
        functools.partial(_attn_kernel, has_cache=has_cache, lam_init=lam_init),
        grid=(b, s // tq), in_specs=in_specs,
        out_specs=[qspec(384), qspec(384)],
        out_shape=[jax.ShapeDtypeStruct((b, s, 384), F32)] * 2,
        compiler_params=_params(("parallel", "parallel")),
        name="attention_cached" if has_cache else "attention",
    )(*args)


def _post_mix_kernel(oa_ref, ob_ref, u_ref, up_ref, un_ref, cbg_ref, x_ref, mod_ref,
                     ga_ref, gb_ref, g64_ref, gc_ref, cw_ref, cbias_ref, wout_ref, o_ref,
                     *, tiles_per_seq):
    i = pl.program_id(0)
    tm = x_ref.shape[0]
    pos = i % tiles_per_seq
    u = u_ref[...]
    row = lax.broadcasted_iota(jnp.int32, u.shape, 0)
    prev_row = jnp.where(pos == 0, 0.0, 1.0) * up_ref[SUBLANES - 1:SUBLANES, :]
    next_row = jnp.where(pos == tiles_per_seq - 1, 0.0, 1.0) * un_ref[0:1, :]
    u_dn = jnp.where(row == 0, prev_row, pltpu.roll(u, 1, 0))
    u_up = jnp.where(row == tm - 1, next_row, pltpu.roll(u, tm - 1, 0))
    y = u_dn * cw_ref[0:1, :] + u * cw_ref[1:2, :] + u_up * cw_ref[2:3, :] + cbias_ref[...]
    oc = cbg_ref[...] * y

    na = _rms(oa_ref[...]) * ga_ref[...]
    ob = ob_ref[...]
    nb = ob * lax.rsqrt(_group_mean_sq(ob, g64_ref) + EPS) * gb_ref[...]
    nc = _rms(oc) * gc_ref[...]
    mix = (_dot(na.astype(BF16), wout_ref[0:384, :]) + _dot(nb.astype(BF16), wout_ref[384:768, :])
           + _dot(nc.astype(BF16), wout_ref[768:1024, :]))
    o_ref[...] = x_ref[...] + mod_ref[2:3, :] * mix


def _post_mix(oa, ob, u, cbg, x, mod_l, row_of_tile, ga, gb, g64, gc, cw, cbias, wout, *, layer, tm, tiles_per_seq):
    t, d = x.shape
    nt = t // tm
    nb8 = t // SUBLANES
    step = tm // SUBLANES
    const = lambda shape: pl.BlockSpec(shape, lambda i: (0,) * len(shape))
    tok = lambda w: pl.BlockSpec((tm, w), lambda i: (i, 0))
    in_specs = [tok(384), tok(384), tok(256),
                pl.BlockSpec((SUBLANES, 256), lambda i: (jnp.maximum(i * step - 1, 0), 0)),
                pl.BlockSpec((SUBLANES, 256), lambda i: (jnp.minimum((i + 1) * step, nb8 - 1), 0)),
                tok(256), tok(d),
                pl.BlockSpec((None, 6, d), lambda i: (row_of_tile(i), 0, 0)),
                const((1, 384)), const((1, 384)), const((384, 384)), const((1, 256)),
                const((3, 256)), const((1, 256)), pl.BlockSpec((None, d, d), lambda i: (layer, 0, 0))]
    return pl.pallas_call(
        functools.partial(_post_mix_kernel, tiles_per_seq=tiles_per_seq),
        grid=(nt,), in_specs=in_specs, out_specs=tok(d),
        out_shape=jax.ShapeDtypeStruct((t, d), F32),
        compiler_params=_params(("parallel",)),
        name="post_mix",
    )(oa, ob, u, u, u, cbg, x, mod_l, ga, gb, g64, gc, cw, cbias, wout)


def _gelu(x):
    k1 = -2.0 * math.sqrt(2.0 / math.pi) * math.log2(math.e)
    return x / (1.0 + jnp.exp2(x * (k1 + (k1 * 0.044715) * (x * x))))


BF16_SUBLANES = 16


def _bcast_row_bf16(row):
    packed = jnp.broadcast_to(row, (BF16_SUBLANES, LANES)).astype(BF16)
    return jnp.concatenate([packed] * (N_KEYS // BF16_SUBLANES), axis=0)


def _top16(s):
    n = s.shape[0]
    rows = lax.broadcasted_iota(jnp.int32, s.shape, 0).astype(F32)
    work = s
    rank = jnp.full(s.shape, float(P_TOPK), F32)
    vals = []
    for a in range(P_TOPK):
        m = jnp.max(work, axis=0, keepdims=True)
        idx = jnp.min(jnp.where(work == m, rows, float(n)), axis=0, keepdims=True)
        hit = rows == idx
        rank = jnp.where(hit, float(a), rank)
        work = jnp.where(hit, NEG_INF, work)
        vals.append(m)
    return vals, rank


def _top16_distinct_pair(s1, s2):
    work1, work2 = s1, s2
    rank2 = jnp.full(s2.shape, float(P_TOPK), F32)
    vals1, vals2 = [], []
    for a in range(P_TOPK):
        m1 = jnp.max(work1, axis=0, keepdims=True)
        m2 = jnp.max(work2, axis=0, keepdims=True)
        hit2 = work2 == m2
        work1 = jnp.where(work1 == m1, NEG_INF, work1)
        rank2 = jnp.where(hit2, float(a), rank2)
        work2 = jnp.where(hit2, NEG_INF, work2)
        vals1.append(m1)
        vals2.append(m2)
    return vals1, vals2, rank2


def _stack16(rows_list):
    w = rows_list[0].shape[1]
    r = lax.broadcasted_iota(jnp.int32, (P_TOPK, w), 0)
    out = jnp.zeros((P_TOPK, w), F32)
    for a, v in enumerate(rows_list):
        out = jnp.where(r == a, v, out)
    return out


_CAND_ROWS = 80


def _cand_build(v1_rows, v1_arr, v2_arr, v2_row0, op):
    pieces = [op(v1_rows[0], v2_arr), op(v1_rows[1], v2_arr[0:8])]
    for a in range(2, 8):
        pieces.append(op(v1_rows[a], v2_arr[0:8]))
    pieces.append(op(v1_arr[8:16], v2_row0))
    return jnp.concatenate(pieces, axis=0)


def _cand_index(w):
    row = lax.broadcasted_iota(jnp.int32, (_CAND_ROWS, w), 0)
    a = jnp.where(row < 16, 0, jnp.where(row < 72, 1 + ((row - 16) >> 3), 8 + (row - 72)))
    b = jnp.where(row < 16, row, jnp.where(row < 72, (row - 16) & 7, 0))
    valid = (a + 1) * (b + 1) <= P_TOPK
    flat = (a * P_TOPK + b).astype(F32)
    return valid, flat


def _peer_head_tables(s1, s2, valid, flat, exact):
    if exact:
        v1_rows, rank1 = _top16(s1)
        v2_rows, rank2 = _top16(s2)
    else:
        v1_rows, v2_rows, rank2 = _top16_distinct_pair(s1, s2)
    v1 = _stack16(v1_rows)
    v2 = _stack16(v2_rows)
    e1 = jnp.exp(v1 - v1_rows[0])
    e2 = jnp.exp(v2 - v2_rows[0])
    e1_rows = [e1[a:a + 1, :] for a in range(8)]
    cand = _cand_build(v1_rows, v1, v2, v2_rows[0], lambda p, q: p + q)
    gcand = _cand_build(e1_rows, e1, e2, e2[0:1, :], lambda p, q: p * q)
    work = jnp.where(valid, cand, NEG_INF)
    sel = jnp.zeros_like(cand)
    for _ in range(P_TOPK):
        m = jnp.max(work, axis=0, keepdims=True)
        idx = jnp.min(jnp.where(work == m, flat, 1e9), axis=0, keepdims=True)
        hit = flat == idx
        sel = jnp.where(hit, 1.0, sel)
        work = jnp.where(hit, NEG_INF, work)
    z = jnp.sum(sel * gcand, axis=0, keepdims=True)
    counts = [jnp.sum(sel[0:16], axis=0, keepdims=True)]
    for a in range(1, 8):
        counts.append(jnp.sum(sel[8 + 8 * a:16 + 8 * a], axis=0, keepdims=True))
    for a in range(8, 16):
        counts.append(sel[64 + a:65 + a, :])
    ltab = jnp.zeros_like(s1)
    if exact:
        member = rank1 < float(P_TOPK)
        for a in range(P_TOPK):
            ltab = jnp.where(rank1 == float(a), counts[a], ltab)
        bad = None
    else:
        member = s1 >= v1_rows[P_TOPK - 1]
        for a in range(P_TOPK):
            ltab = jnp.where(s1 == v1_rows[a], counts[a], ltab)
        n1 = jnp.sum(jnp.where(member, 1.0, 0.0), axis=0, keepdims=True)
        n2 = jnp.sum(jnp.where(rank2 < float(P_TOPK), 1.0, 0.0), axis=0, keepdims=True)
        bad = jnp.where(n1 + n2 != 2.0 * P_TOPK, 1.0, 0.0)
    atab = jnp.where(member, jnp.exp(s1 - v1_rows[0]) * (1.0 / z), 0.0)
    btab = jnp.exp(s2 - v2_rows[0])
    return ltab, atab, rank2, btab, bad


def _peer_prep(x_ref, mod_ref, n2g_ref, wqt_ref, ht_scr, qt_scr):
    tm = x_ref.shape[0]
    ncb = tm // LANES
    nq = wqt_ref.shape[0]
    x = x_ref[...]
    h = _rms(x) * n2g_ref[...] * (1.0 + mod_ref[4:5, :]) + mod_ref[3:4, :]
    htb = h.T.astype(BF16)
    ht_scr[...] = htb
    for half in range(2):
        rs = slice(half * (nq // 2), (half + 1) * (nq // 2))
        qt = _dot(wqt_ref[rs, :], htb)
        for cb in range(ncb):
            qt_scr[cb, rs, :] = qt[:, cb * LANES:(cb + 1) * LANES]


def _peer_head_block(hd, cb, keys_ref, qt_scr, tabs, exact):
    rank2_scr, b_scr, l_scr, a_scr = tabs
    ncb = qt_scr.shape[0]
    valid, flat = _cand_index(LANES)
    r1 = pl.multiple_of(hd * (2 * P_HALF), 2 * P_HALF)
    q1 = qt_scr[cb, pl.ds(r1, P_HALF), :].astype(BF16)
    q2 = qt_scr[cb, pl.ds(r1 + P_HALF, P_HALF), :].astype(BF16)
    s1 = _dot(keys_ref[2 * hd], q1)
    s2 = _dot(keys_ref[2 * hd + 1], q2)
    ltab, atab, rank2, btab, bad = _peer_head_tables(s1, s2, valid, flat, exact)
    n = hd * ncb + cb
    l_scr[n] = ltab
    a_scr[n] = atab
    rank2_scr[n] = rank2.astype(rank2_scr.dtype)
    b_scr[n] = btab.astype(b_scr.dtype)
    return bad


def _peer_head(hd, keys_ref, qt_scr, tabs, bad_scr, exact):
    bad = jnp.zeros((1, LANES), F32)
    for cb in range(qt_scr.shape[0]):
        bad_cb = _peer_head_block(hd, cb, keys_ref, qt_scr, tabs, exact)
        if not exact:
            bad = jnp.maximum(bad, bad_cb)
    if not exact:
        bad_scr[hd] = jnp.broadcast_to(bad, (SUBLANES, LANES))


def _peer_head_redo(hd, keys_ref, qt_scr, tabs, bad_scr):
    @pl.when(jnp.max(bad_scr[hd]) > 0.0)
    def _():
        _peer_head(hd, keys_ref, qt_scr, tabs, bad_scr, exact=True)


PEER_SUB = 512
MXU_COLS = 256
PEER_VMEM_LIMIT = 56 * 1024 * 1024


def _peer_kernel(x_ref, mod_ref, n2g_ref, wqt_ref, keys_ref, u_ref, vt_ref, o_ref,
                 ht_scr, qt_scr, rank2_scr, b_scr, l_scr, a_scr, acc_scr, bad_scr):
    j = pl.program_id(1)
    nj = pl.num_programs(1)
    tm = x_ref.shape[0]
    ncb = tm // LANES
    ec = u_ref.shape[0]
    rows_per_sub = PEER_SUB // N_KEYS
    tabs = (rank2_scr, b_scr, l_scr, a_scr)

    @pl.when(j == 0)
    def _():
        acc_scr[...] = jnp.zeros_like(acc_scr)
        _peer_prep(x_ref, mod_ref, n2g_ref, wqt_ref, ht_scr, qt_scr)

        def head_body(hd, carry):
            _peer_head(hd, keys_ref, qt_scr, tabs, bad_scr, exact=False)
            return carry

        def redo_body(hd, carry):
            _peer_head_redo(hd, keys_ref, qt_scr, tabs, bad_scr)
            return carry

        lax.fori_loop(0, P_HEADS, head_body, 0)
        lax.fori_loop(0, P_HEADS, redo_body, 0)

    n_half = 2
    eh = ec // n_half
    for half in range(n_half):
        for nb in range(tm // MXU_COLS):
            ns = slice(nb * MXU_COLS, (nb + 1) * MXU_COLS)
            act = _dot(u_ref[half * eh:(half + 1) * eh, :], ht_scr[:, ns])
            g_rows = []
            for q in range(eh // PEER_SUB):
                cols = []
                for c2 in range(MXU_COLS // LANES):
                    cb = nb * (MXU_COLS // LANES) + c2
                    ws = [jnp.zeros((N_KEYS, LANES), BF16) for _ in range(rows_per_sub)]
                    for hd in range(P_HEADS):
                        n = hd * ncb + cb
                        rank2 = rank2_scr[n]
                        btab = b_scr[n]
                        for r in range(rows_per_sub):
                            i1 = j * (ec // N_KEYS) + (half * eh + q * PEER_SUB) // N_KEYS + r
                            lrow = _bcast_row_bf16(l_scr[n, pl.ds(i1, 1), :])
                            arow = _bcast_row_bf16(a_scr[n, pl.ds(i1, 1), :])
                            ws[r] = ws[r] + jnp.where(rank2 < lrow, btab * arow, jnp.zeros_like(btab))
                    gs = []
                    for r in range(rows_per_sub):
                        r0 = q * PEER_SUB + r * N_KEYS
                        a_blk = act[r0:r0 + N_KEYS, c2 * LANES:(c2 + 1) * LANES]
                        gs.append(ws[r] * _gelu(a_blk.astype(BF16)))
                    cols.append(jnp.concatenate(gs, axis=0))
                g_rows.append(jnp.concatenate(cols, axis=1))
            g = jnp.concatenate(g_rows, axis=0)
            acc_scr[:, ns] += _dot(vt_ref[:, half * eh:(half + 1) * eh], g)

    @pl.when(j == nj - 1)
    def _():
        o_ref[...] = x_ref[...] + mod_ref[5:6, :] * acc_scr[...].T


def _peer(x, mod_l, row_of_tile, n2g, wqt, keys, u_tab, vt_tab, *, layer, tm, ec):
    t, d = x.shape
    nt = t // tm
    ne = u_tab.shape[1]
    ncb = tm // LANES
    nq = wqt.shape[1]
    tabs = lambda dt: pltpu.VMEM((P_HEADS * ncb, N_KEYS, LANES), dt)
    return pl.pallas_call(
        _peer_kernel,
        grid=(nt, ne // ec),
        in_specs=[pl.BlockSpec((tm, d), lambda i, j: (i, 0)),
                  pl.BlockSpec((None, 6, d), lambda i, j: (row_of_tile(i), 0, 0)),
                  pl.BlockSpec((1, d), lambda i, j: (0, 0)),
                  pl.BlockSpec((None, nq, d), lambda i, j: (layer, 0, 0), pipeline_mode=pl.Buffered(1)),
                  pl.BlockSpec((None, 2 * P_HEADS, N_KEYS, P_HALF), lambda i, j: (layer, 0, 0, 0),
                               pipeline_mode=pl.Buffered(1)),
                  pl.BlockSpec((None, ec, d), lambda i, j: (layer, j, 0)),
                  pl.BlockSpec((None, d, ec), lambda i, j: (layer, 0, j))],
        out_specs=pl.BlockSpec((tm, d), lambda i, j: (i, 0)),
        out_shape=jax.ShapeDtypeStruct((t, d), F32),
        scratch_shapes=[pltpu.VMEM((d, tm), BF16),
                        pltpu.VMEM((ncb, nq, LANES), F32),
                        tabs(BF16), tabs(BF16), tabs(F32), tabs(F32),
                        pltpu.VMEM((d, tm), F32),
                        pltpu.VMEM((P_HEADS, SUBLANES, LANES), F32)],
        compiler_params=_params(("parallel", "arbitrary"), PEER_VMEM_LIMIT),
        name="peer",
    )(x, mod_l, n2g, wqt, keys, u_tab, vt_tab)


def _rope_tables(n_tokens, dim):
    quarter = dim // 4
    pos = np.arange(n_tokens)
    row = (pos // GRID_W).astype(np.float32)
    col = (pos % GRID_W).astype(np.float32)
    lane = np.arange(LANES)
    dd = lane % dim
    half = dd // (dim // 2)
    e = dd % (dim // 2)
    second = e // quarter
    freq = e % quarter
    inv = jnp.asarray(ROPE_BASE, F32) ** (-jnp.arange(quarter, dtype=F32) / quarter)
    inv_lane = inv[freq]
    p = jnp.where(jnp.asarray(half == 0)[None, :], jnp.asarray(row)[:, None], jnp.asarray(col)[:, None])
    ang = p * inv_lane[None, :]
    cos, sin = jnp.cos(ang), jnp.sin(ang)
    first = jnp.asarray(second == 0)[None, :]
    return cos, jnp.where(first, -sin, 0.0), jnp.where(first, 0.0, sin)


def _block_diag_mean(width, group):
    idx = np.arange(width) // group
    return jnp.asarray((idx[:, None] == idx[None, :]).astype(np.float32) / group, BF16)


def kernel(x_prompt, x_sample, cache_a_k, cache_a_v, cache_b_k, cache_b_v, c, c_ctx, ada_w, ada_b, norm1_g, norm2_g, w_in, qn_a, kn_a, qn_b, kn_b, lam_q1, lam_k1, lam_q2, lam_k2, subln_g, out_norm_a, out_norm_c, conv_w, conv_b, w_out, peer_wq, peer_keys, peer_u, peer_v):
    nb, s_p, d = x_prompt.shape
    db, s_l, _ = x_sample.shape
    depth = w_in.shape[0]
    past = cache_a_k.shape[2]
    tm_p = s_p
    tm_l = min(MIX_TILE, s_l)
    tq_l = min(256, s_l)
    tm_pre_p = min(MIX_TILE, nb * s_p)
    tm_peer = TOKEN_TILE
    ec = 2048
    assert s_l % tm_l == 0 and s_l % GRID_W == 0 and d == 1024 and (nb * s_p) % tm_pre_p == 0
    assert s_l % tm_peer == 0 and (nb * s_p) % tm_peer == 0 and tm_p % SUBLANES == 0

    n_cond = 1 + db
    n_rows = -(-n_cond // SUBLANES) * SUBLANES
    cond = jnp.concatenate([c_ctx[None, :], c, jnp.zeros((n_rows - n_cond, d), F32)], axis=0)
    mod = _modulation(cond, ada_w, ada_b).reshape(depth, n_rows, 6, d)

    def split_cols(w):
        offs = np.cumsum([0, 384, 128, 128, 384, 384, 384, 256, 256, 256])
        return [w[..., offs[k]:offs[k + 1]] for k in range(9)]

    def swap_kv(w):
        return jnp.concatenate([w[..., 64:128], w[..., 0:64]], axis=-1)

    wqa, wka, wva, wqb, wkb, wvb, wci, wcb, wcc = split_cols(w_in)
    win = jnp.concatenate([wqa, wka, swap_kv(wka), wva, swap_kv(wva), wqb, wkb, wvb, wci, wcb, wcc],
                          axis=-1).astype(BF16)
    wout = w_out.astype(BF16)
    wqt = jnp.swapaxes(peer_wq, 1, 2).astype(BF16)
    keys = peer_keys.reshape(depth, 2 * P_HEADS, N_KEYS, P_HALF).astype(BF16)
    u_tab = peer_u.astype(BF16)
    vt_tab = jnp.swapaxes(peer_v, 1, 2).astype(BF16)

    g64 = _block_diag_mean(384, HD)
    g32 = _block_diag_mean(384, B_DK)
    rope_a = _rope_tables(s_l, HD)
    rope_b = _rope_tables(s_l, B_DK)

    ck = cache_a_k.reshape(db, depth, past, 128)
    cv = cache_a_v.reshape(db, depth, past, 128)
    caches = (jnp.concatenate([ck, swap_kv(ck)], axis=-1).astype(BF16),
              jnp.concatenate([cv, swap_kv(cv)], axis=-1).astype(BF16),
              cache_b_k.reshape(db, depth, past, 384).astype(BF16),
              cache_b_v.reshape(db, depth, past, 384).astype(BF16))

    xp = x_prompt.reshape(nb * s_p, d)
    xs = x_sample.reshape(db * s_l, d)
    tps_p = s_p // tm_p
    tps_l = s_l // tm_l
    row_p = lambda i: 0
    row_l = lambda i: 1 + i // tps_l
    row_l_peer = lambda i: 1 + i // (s_l // tm_peer)

    new = [[], [], [], []]
    for l in range(depth):
        lam_init = 0.8 - 0.6 * math.exp(-0.3 * l)
        gains = ((jnp.tile(qn_a[l], A_HEADS) * HD ** -0.5)[None, :],
                 jnp.tile(kn_a[l], 2 * A_KV)[None, :],
                 (jnp.tile(qn_b[l], 2 * B_HEADS) * B_DK ** -0.5)[None, :],
                 jnp.tile(kn_b[l], 2 * B_HEADS)[None, :])
        lams = (lam_q1[l][None, :], lam_k1[l][None, :], lam_q2[l][None, :], lam_k2[l][None, :])
        gb = (jnp.tile(subln_g[l], B_HEADS) * (1.0 - lam_init))[None, :]
        post_args = (out_norm_a[l][None, :], gb, g64, out_norm_c[l][None, :], conv_w[l], conv_b[l][None, :],
                     wout)
        n1g = norm1_g[l][None, :]
        n2g = norm2_g[l][None, :]

        qa, ka2, va2, qb, kb, vb, u, cbg, cka, cva, ckb, cvb = _pre_mix(
            xp, mod[l], row_p, n1g, win, gains, (g64, g32), None, layer=l,
            tm=tm_pre_p, tiles_per_seq=1, emit_cache=True)
        new[0].append(cka.reshape(nb, s_p, A_KV, HD))
        new[1].append(cva.reshape(nb, s_p, A_KV, HD))
        new[2].append(ckb.reshape(nb, s_p, B_HEADS, 2, B_DK))
        new[3].append(cvb.reshape(nb, s_p, B_HEADS, 2 * B_DK))
        r3 = lambda a: a.reshape(nb, s_p, a.shape[-1])
        oa, ob = _attention(r3(qa), r3(ka2), r3(va2), r3(qb), r3(kb), r3(vb), None, lams,
                            layer=l, lam_init=lam_init, tq=s_p)
        xp = _post_mix(oa.reshape(nb * s_p, 384), ob.reshape(nb * s_p, 384), u, cbg, xp, mod[l], row_p,
                       *post_args, layer=l, tm=tm_p, tiles_per_seq=tps_p)
        xp = _peer(xp, mod[l], row_p, n2g, wqt, keys, u_tab, vt_tab, layer=l, tm=tm_peer, ec=ec)

        qa, ka2, va2, qb, kb, vb, u, cbg = _pre_mix(
            xs, mod[l], row_l, n1g, win, gains, (g64, g32), rope_a + rope_b, layer=l,
            tm=tm_l, tiles_per_seq=tps_l, emit_cache=False)
        r3 = lambda a: a.reshape(db, s_l, a.shape[-1])
        oa, ob = _attention(r3(qa), r3(ka2), r3(va2), r3(qb), r3(kb), r3(vb), caches, lams,
                            layer=l, lam_init=lam_init, tq=tq_l)
        xs = _post_mix(oa.reshape(db * s_l, 384), ob.reshape(db * s_l, 384), u, cbg, xs, mod[l], row_l,
                       *post_args, layer=l, tm=tm_l, tiles_per_seq=tps_l)
        xs = _peer(xs, mod[l], row_l_peer, n2g, wqt, keys, u_tab, vt_tab, layer=l, tm=tm_peer, ec=ec)

    return (xp.reshape(nb, s_p, d), xs.reshape(db, s_l, d),
            jnp.stack(new[0], axis=1), jnp.stack(new[1], axis=1),
            jnp.stack(new[2], axis=1), jnp.stack(new[3], axis=1))
```

```python
import functools
import math

import numpy as np
import jax
import jax.numpy as jnp
from jax import lax
from jax.experimental import pallas as pl
from jax.experimental.pallas import tpu as pltpu

F32 = jnp.float32
BF16 = jnp.bfloat16

HD = 64
A_HEADS = 6
A_KV = 2
B_HEADS = 6
B_DK = 32
C_WIDTH = 256
A_WIDTH = A_HEADS * HD
B_WIDTH = B_HEADS * 2 * B_DK
GRID_W = 64
ROPE_BASE = 10000.0
EPS = 1e-6
P_HEADS = 8
N_KEYS = 128
P_HALF = 128
P_TOPK = 16

LANES = 128
SUBLANES = 8
VMEM_LIMIT = 48 * 1024 * 1024
TOKEN_TILE = 512
MIX_TILE = 1024
ATTN_Q_TILE = 128

_QA = (0, 384)
_KA2 = (384, 640)
_VA2 = (640, 896)
_QB = (896, 1280)
_KB = (1280, 1664)
_VB = (1664, 2048)
_CI = (2048, 2304)
_CB = (2304, 2560)
_CC = (2560, 2816)
NP_COLS = 2816

NEG_INF = float("-inf")


def _params(sem, vmem=VMEM_LIMIT):
    return pltpu.CompilerParams(dimension_semantics=sem, vmem_limit_bytes=vmem)


def _dot(a, b):
    return jnp.dot(a, b, preferred_element_type=F32)


def _dot_nt(a, b):
    return lax.dot_general(a, b, (((1,), (1,)), ((), ())), preferred_element_type=F32)


def _split3(a):
    hi = a.astype(BF16)
    r = a - hi.astype(F32)
    mid = r.astype(BF16)
    lo = (r - mid.astype(F32)).astype(BF16)
    return hi, mid, lo


def _dot_hp(a, b):
    a0, a1, a2 = _split3(a)
    b0, b1, b2 = _split3(b)
    return (_dot(a0, b0) + (_dot(a0, b1) + _dot(a1, b0))
            + (_dot(a0, b2) + _dot(a1, b1) + _dot(a2, b0)))


def _group_mean_sq(v, g_ref):
    v2 = v * v
    hi = v2.astype(BF16)
    lo = (v2 - hi.astype(F32)).astype(BF16)
    g = g_ref[...]
    return _dot(hi, g) + _dot(lo, g)


def _rms(x):
    return x * lax.rsqrt(jnp.mean(x * x, axis=-1, keepdims=True) + EPS)


def _mod_kernel(cond_ref, w_ref, b_ref, o_ref):
    c = cond_ref[...]
    s = c / (1.0 + jnp.exp(-c))
    o_ref[...] = _dot_hp(s, w_ref[...]) + b_ref[...]


def _modulation(cond, ada_w, ada_b):
    depth, d, d6 = ada_w.shape
    n = cond.shape[0]
    tn = 1536
    return pl.pallas_call(
        _mod_kernel,
        grid=(depth, d6 // tn),
        in_specs=[pl.BlockSpec((n, d), lambda l, j: (0, 0)),
                  pl.BlockSpec((None, d, tn), lambda l, j: (l, 0, j)),
                  pl.BlockSpec((None, 1, tn), lambda l, j: (l, 0, j))],
        out_specs=pl.BlockSpec((None, n, tn), lambda l, j: (l, 0, j)),
        out_shape=jax.ShapeDtypeStruct((depth, n, d6), F32),
        compiler_params=_params(("parallel", "parallel")),
        name="modulation",
    )(cond, ada_w, ada_b.reshape(depth, 1, d6))


def _rope_block(blk, c, sm, sp, shift):
    return (blk * c + pltpu.roll(blk, LANES - shift, 1) * sm + pltpu.roll(blk, shift, 1) * sp)


def _pre_mix_kernel(*refs, rope, emit_cache):
    it = iter(refs)
    x_ref, mod_ref, n1g_ref, win_ref = next(it), next(it), next(it), next(it)
    gqa_ref, gka_ref, gqb_ref, gkb_ref = next(it), next(it), next(it), next(it)
    g64_ref, g32_ref = next(it), next(it)
    if rope:
        ca_ref, sma_ref, spa_ref = next(it), next(it), next(it)
        cb_ref, smb_ref, spb_ref = next(it), next(it), next(it)
    qa_o, ka_o, va_o, qb_o, kb_o, vb_o, u_o, cbg_o = (next(it) for _ in range(8))
    if emit_cache:
        cka_o, cva_o, ckb_o, cvb_o = (next(it) for _ in range(4))

    x = x_ref[...]
    sh1 = mod_ref[0:1, :]
    sc1 = mod_ref[1:2, :]
    h = _rms(x) * n1g_ref[...] * (1.0 + sc1) + sh1
    hb = h.astype(BF16)

    def proj(seg):
        return _dot(hb, win_ref[:, seg[0]:seg[1]])

    def qk(seg, g_ref, gain_ref, width):
        p = proj(seg)
        ms = _group_mean_sq(p, g_ref) if width == 384 else _dot_group256(p, g_ref)
        return p * lax.rsqrt(ms + EPS), gain_ref[...]

    def _dot_group256(p, g_ref):
        v2 = p * p
        hi = v2.astype(BF16)
        lo = (v2 - hi.astype(F32)).astype(BF16)
        g = g_ref[0:256, 0:256]
        return _dot(hi, g) + _dot(lo, g)

    def finish(y, gain, out_ref, tabs, shift, cache_ref=None, cache_cols=None):
        nblk = y.shape[1] // LANES
        for m in range(nblk):
            sl = slice(m * LANES, (m + 1) * LANES)
            blk = y[:, sl] * gain[:, sl]
            if cache_ref is not None and m < cache_cols // LANES:
                cache_ref[:, sl] = blk
            if tabs is not None:
                blk = _rope_block(blk, tabs[0][...], tabs[1][...], tabs[2][...], shift)
            out_ref[:, sl] = blk.astype(out_ref.dtype)

    tabs_a = (ca_ref, sma_ref, spa_ref) if rope else None
    tabs_b = (cb_ref, smb_ref, spb_ref) if rope else None

    y, g = qk(_QA, g64_ref, gqa_ref, 384)
    finish(y, g, qa_o, tabs_a, HD // 4)
    y, g = qk(_KA2, g64_ref, gka_ref, 256)
    finish(y, g, ka_o, tabs_a, HD // 4, cka_o if emit_cache else None, 128)
    va = proj(_VA2)
    va_o[...] = va.astype(va_o.dtype)
    if emit_cache:
        cva_o[...] = va[:, 0:128]
    y, g = qk(_QB, g32_ref, gqb_ref, 384)
    finish(y, g, qb_o, tabs_b, B_DK // 4)
    y, g = qk(_KB, g32_ref, gkb_ref, 384)
    finish(y, g, kb_o, tabs_b, B_DK // 4, ckb_o if emit_cache else None, 384)
    vb = proj(_VB)
    vb_o[...] = vb.astype(vb_o.dtype)
    if emit_cache:
        cvb_o[...] = vb
    u_o[...] = proj(_CC) * proj(_CI)
    cbg_o[...] = proj(_CB)


def _pre_mix(x, mod_l, row_of_tile, n1g, win, gains, gmats, rope_tabs, *, layer, tm, tiles_per_seq, emit_cache):
    t, d = x.shape
    nt = t // tm
    rope = rope_tabs is not None
    const = lambda shape: pl.BlockSpec(shape, lambda i: (0,) * len(shape))
    in_specs = [pl.BlockSpec((tm, d), lambda i: (i, 0)),
                pl.BlockSpec((None, 6, d), lambda i: (row_of_tile(i), 0, 0)),
                const((1, d)), pl.BlockSpec((None, d, NP_COLS), lambda i: (layer, 0, 0)),
                const((1, 384)), const((1, 256)), const((1, 384)), const((1, 384)),
                const((384, 384)), const((384, 384))]
    args = [x, mod_l, n1g, win, *gains, *gmats]
    if rope:
        in_specs += [pl.BlockSpec((tm, LANES), lambda i: (i % tiles_per_seq, 0))] * 6
        args += list(rope_tabs)
    widths = [(384, BF16), (256, BF16), (256, BF16), (384, BF16), (384, BF16), (384, BF16),
              (256, F32), (256, F32)]
    if emit_cache:
        widths += [(128, F32), (128, F32), (384, F32), (384, F32)]
    out_shape = [jax.ShapeDtypeStruct((t, w), dt) for w, dt in widths]
    out_specs = [pl.BlockSpec((tm, w), lambda i: (i, 0)) for w, _ in widths]
    return pl.pallas_call(
        functools.partial(_pre_mix_kernel, rope=rope, emit_cache=emit_cache),
        grid=(nt,), in_specs=in_specs, out_specs=out_specs, out_shape=out_shape,
        compiler_params=_params(("parallel",)),
        name="pre_mix_rope" if rope else "pre_mix",
    )(*args)


def _softmax_pv(q, k_parts, v_parts):
    ss = [_dot_nt(q, k) for k in k_parts]
    m = jnp.max(ss[0], axis=-1, keepdims=True)
    for s in ss[1:]:
        m = jnp.maximum(m, jnp.max(s, axis=-1, keepdims=True))
    l = None
    o = None
    for s, v in zip(ss, v_parts):
        p = jnp.exp(s - m)
        ls = jnp.sum(p, axis=-1, keepdims=True)
        os_ = _dot(p.astype(BF16), v)
        l = ls if l is None else l + ls
        o = os_ if o is None else o + os_
    return o * (1.0 / l)


def _attn_kernel(*refs, has_cache, lam_init):
    it = iter(refs)
    qa_ref, ka_ref, va_ref, qb_ref, kb_ref, vb_ref = (next(it) for _ in range(6))
    if has_cache:
        cka_ref, cva_ref, ckb_ref, cvb_ref = (next(it) for _ in range(4))
    lq1_ref, lk1_ref, lq2_ref, lk2_ref = (next(it) for _ in range(4))
    oa_ref, ob_ref = next(it), next(it)

    tq = qa_ref.shape[0]
    lane = lax.broadcasted_iota(jnp.int32, (tq, LANES), 1)
    lo_half = lane < 64

    lam = (jnp.exp(jnp.sum(lq1_ref[...] * lk1_ref[...], axis=-1, keepdims=True))
           - jnp.exp(jnp.sum(lq2_ref[...] * lk2_ref[...], axis=-1, keepdims=True)) + lam_init)

    o_a = [None] * A_HEADS
    for kblk in range(2):
        heads = [h for h in range(A_HEADS) if (0 if h // (A_HEADS // A_KV) == h % 2 else 1) == kblk]
        ksl = slice(kblk * LANES, (kblk + 1) * LANES)
        qs = []
        for h in heads:
            qblk = qa_ref[:, (h // 2) * LANES:(h // 2 + 1) * LANES]
            keep = lo_half if h % 2 == 0 else jnp.logical_not(lo_half)
            qs.append(jnp.where(keep, qblk, jnp.zeros_like(qblk)))
        ks, vs = [ka_ref[:, ksl]], [va_ref[:, ksl]]
        if has_cache:
            ks.append(cka_ref[:, ksl])
            vs.append(cva_ref[:, ksl])
        o = _softmax_pv(jnp.concatenate(qs, axis=0), ks, vs)
        for k, h in enumerate(heads):
            o_a[h] = o[k * tq:(k + 1) * tq, :]
    for m in range(A_HEADS // 2):
        oa_ref[:, m * LANES:(m + 1) * LANES] = jnp.where(lo_half, o_a[2 * m], o_a[2 * m + 1])

    seg_id = lane >> 5
    for m in range(B_HEADS // 2):
        sl = slice(m * LANES, (m + 1) * LANES)
        qblk = qb_ref[:, sl]
        ks, vs = [kb_ref[:, sl]], [vb_ref[:, sl]]
        if has_cache:
            ks.append(ckb_ref[:, sl])
            vs.append(cvb_ref[:, sl])
        qs = [jnp.where(seg_id == seg, qblk, jnp.zeros_like(qblk)) for seg in range(4)]
        o = _softmax_pv(jnp.concatenate(qs, axis=0), ks, vs)
        outs = [o[(2 * par) * tq:(2 * par + 1) * tq, :] - lam * o[(2 * par + 1) * tq:(2 * par + 2) * tq, :]
                for par in range(2)]
        ob_ref[:, sl] = jnp.where(lo_half, outs[0], outs[1])


def _attention(qa, ka2, va2, qb, kb, vb, caches, lams, *, layer, lam_init, tq):
    b, s, _ = qa.shape
    has_cache = caches is not None
    qspec = lambda w: pl.BlockSpec((None, tq, w), lambda i, j: (i, j, 0))
    kspec = lambda w: pl.BlockSpec((None, s, w), lambda i, j: (i, 0, 0))
    in_specs = [qspec(384), kspec(256), kspec(256), qspec(384), kspec(384), kspec(384)]
    args = [qa, ka2, va2, qb, kb, vb]
    if has_cache:
        lc = caches[0].shape[2]
        cspec = lambda w: pl.BlockSpec((None, None, lc, w), lambda i, j: (i, layer, 0, 0))
        in_specs += [cspec(256), cspec(256), cspec(384), cspec(384)]
        args += list(caches)
    in_specs += [pl.BlockSpec((1, B_DK), lambda i, j: (0, 0))] * 4
    args += list(lams)
    return pl.pallas_call(
        functools.partial(_attn_kernel, has_cache=has_cache, lam_init=lam_init),
        grid=(b, s // tq), in_specs=in_specs,
        out_specs=[qspec(384), qspec(384)],
        out_shape=[jax.ShapeDtypeStruct((b, s, 384), F32)] * 2,
        compiler_params=_params(("parallel", "parallel")),
        name="attention_cached" if has_cache else "attention",
    )(*args)


def _post_mix_kernel(oa_ref, ob_ref, u_ref, up_ref, un_ref, cbg_ref, x_ref, mod_ref,
                     ga_ref, gb_ref, g64_ref, gc_ref, cw_ref, cbias_ref, wout_ref, o_ref,
                     *, tiles_per_seq):
    i = pl.program_id(0)
    tm = x_ref.shape[0]
    pos = i % tiles_per_seq
    u = u_ref[...]
    row = lax.broadcasted_iota(jnp.int32, u.shape, 0)
    prev_row = jnp.where(pos == 0, 0.0, 1.0) * up_ref[SUBLANES - 1:SUBLANES, :]
    next_row = jnp.where(pos == tiles_per_seq - 1, 0.0, 1.0) * un_ref[0:1, :]
    u_dn = jnp.where(row == 0, prev_row, pltpu.roll(u, 1, 0))
    u_up = jnp.where(row == tm - 1, next_row, pltpu.roll(u, tm - 1, 0))
    y = u_dn * cw_ref[0:1, :] + u * cw_ref[1:2, :] + u_up * cw_ref[2:3, :] + cbias_ref[...]
    oc = cbg_ref[...] * y

    na = _rms(oa_ref[...]) * ga_ref[...]
    ob = ob_ref[...]
    nb = ob * lax.rsqrt(_group_mean_sq(ob, g64_ref) + EPS) * gb_ref[...]
    nc = _rms(oc) * gc_ref[...]
    mix = (_dot(na.astype(BF16), wout_ref[0:384, :]) + _dot(nb.astype(BF16), wout_ref[384:768, :])
           + _dot(nc.astype(BF16), wout_ref[768:1024, :]))
    o_ref[...] = x_ref[...] + mod_ref[2:3, :] * mix


def _post_mix(oa, ob, u, cbg, x, mod_l, row_of_tile, ga, gb, g64, gc, cw, cbias, wout, *, layer, tm, tiles_per_seq):
    t, d = x.shape
    nt = t // tm
    nb8 = t // SUBLANES
    step = tm // SUBLANES
    const = lambda shape: pl.BlockSpec(shape, lambda i: (0,) * len(shape))
    tok = lambda w: pl.BlockSpec((tm, w), lambda i: (i, 0))
    in_specs = [tok(384), tok(384), tok(256),
                pl.BlockSpec((SUBLANES, 256), lambda i: (jnp.maximum(i * step - 1, 0), 0)),
                pl.BlockSpec((SUBLANES, 256), lambda i: (jnp.minimum((i + 1) * step, nb8 - 1), 0)),
                tok(256), tok(d),
                pl.BlockSpec((None, 6, d), lambda i: (row_of_tile(i), 0, 0)),
                const((1, 384)), const((1, 384)), const((384, 384)), const((1, 256)),
                const((3, 256)), const((1, 256)), pl.BlockSpec((None, d, d), lambda i: (layer, 0, 0))]
    return pl.pallas_call(
        functools.partial(_post_mix_kernel, tiles_per_seq=tiles_per_seq),
        grid=(nt,), in_specs=in_specs, out_specs=tok(d),
        out_shape=jax.ShapeDtypeStruct((t, d), F32),
        compiler_params=_params(("parallel",)),
        name="post_mix",
    )(oa, ob, u, u, u, cbg, x, mod_l, ga, gb, g64, gc, cw, cbias, wout)


def _gelu(x):
    k1 = -2.0 * math.sqrt(2.0 / math.pi) * math.log2(math.e)
    return x / (1.0 + jnp.exp2(x * (k1 + (k1 * 0.044715) * (x * x))))


BF16_SUBLANES = 16


def _bcast_row_bf16(row):
    packed = jnp.broadcast_to(row, (BF16_SUBLANES, LANES)).astype(BF16)
    return jnp.concatenate([packed] * (N_KEYS // BF16_SUBLANES), axis=0)


def _top16(s):
    n = s.shape[0]
    rows = lax.broadcasted_iota(jnp.int32, s.shape, 0).astype(F32)
    work = s
    rank = jnp.full(s.shape, float(P_TOPK), F32)
    vals = []
    for a in range(P_TOPK):
        m = jnp.max(work, axis=0, keepdims=True)
        idx = jnp.min(jnp.where(work == m, rows, float(n)), axis=0, keepdims=True)
        hit = rows == idx
        rank = jnp.where(hit, float(a), rank)
        work = jnp.where(hit, NEG_INF, work)
        vals.append(m)
    return vals, rank


def _top16_distinct_pair(s1, s2):
    work1, work2 = s1, s2
    rank2 = jnp.full(s2.shape, float(P_TOPK), F32)
    vals1, vals2 = [], []
    for a in range(P_TOPK):
        m1 = jnp.max(work1, axis=0, keepdims=True)
        m2 = jnp.max(work2, axis=0, keepdims=True)
        hit2 = work2 == m2
        work1 = jnp.where(work1 == m1, NEG_INF, work1)
        rank2 = jnp.where(hit2, float(a), rank2)
        work2 = jnp.where(hit2, NEG_INF, work2)
        vals1.append(m1)
        vals2.append(m2)
    return vals1, vals2, rank2


def _stack16(rows_list):
    w = rows_list[0].shape[1]
    r = lax.broadcasted_iota(jnp.int32, (P_TOPK, w), 0)
    out = jnp.zeros((P_TOPK, w), F32)
    for a, v in enumerate(rows_list):
        out = jnp.where(r == a, v, out)
    return out


_CAND_ROWS = 80


def _cand_build(v1_rows, v1_arr, v2_arr, v2_row0, op):
    pieces = [op(v1_rows[0], v2_arr), op(v1_rows[1], v2_arr[0:8])]
    for a in range(2, 8):
        pieces.append(op(v1_rows[a], v2_arr[0:8]))
    pieces.append(op(v1_arr[8:16], v2_row0))
    return jnp.concatenate(pieces, axis=0)


def _cand_index(w):
    row = lax.broadcasted_iota(jnp.int32, (_CAND_ROWS, w), 0)
    a = jnp.where(row < 16, 0, jnp.where(row < 72, 1 + ((row - 16) >> 3), 8 + (row - 72)))
    b = jnp.where(row < 16, row, jnp.where(row < 72, (row - 16) & 7, 0))
    valid = (a + 1) * (b + 1) <= P_TOPK
    flat = (a * P_TOPK + b).astype(F32)
    return valid, flat


def _peer_head_tables(s1, s2, valid, flat, exact):
    if exact:
        v1_rows, rank1 = _top16(s1)
        v2_rows, rank2 = _top16(s2)
    else:
        v1_rows, v2_rows, rank2 = _top16_distinct_pair(s1, s2)
    v1 = _stack16(v1_rows)
    v2 = _stack16(v2_rows)
    e1 = jnp.exp(v1 - v1_rows[0])
    e2 = jnp.exp(v2 - v2_rows[0])
    e1_rows = [e1[a:a + 1, :] for a in range(8)]
    cand = _cand_build(v1_rows, v1, v2, v2_rows[0], lambda p, q: p + q)
    gcand = _cand_build(e1_rows, e1, e2, e2[0:1, :], lambda p, q: p * q)
    work = jnp.where(valid, cand, NEG_INF)
    sel = jnp.zeros_like(cand)
    for _ in range(P_TOPK):
        m = jnp.max(work, axis=0, keepdims=True)
        idx = jnp.min(jnp.where(work == m, flat, 1e9), axis=0, keepdims=True)
        hit = flat == idx
        sel = jnp.where(hit, 1.0, sel)
        work = jnp.where(hit, NEG_INF, work)
    z = jnp.sum(sel * gcand, axis=0, keepdims=True)
    counts = [jnp.sum(sel[0:16], axis=0, keepdims=True)]
    for a in range(1, 8):
        counts.append(jnp.sum(sel[8 + 8 * a:16 + 8 * a], axis=0, keepdims=True))
    for a in range(8, 16):
        counts.append(sel[64 + a:65 + a, :])
    ltab = jnp.zeros_like(s1)
    if exact:
        member = rank1 < float(P_TOPK)
        for a in range(P_TOPK):
            ltab = jnp.where(rank1 == float(a), counts[a], ltab)
        bad = None
    else:
        member = s1 >= v1_rows[P_TOPK - 1]
        for a in range(P_TOPK):
            ltab = jnp.where(s1 == v1_rows[a], counts[a], ltab)
        n1 = jnp.sum(jnp.where(member, 1.0, 0.0), axis=0, keepdims=True)
        n2 = jnp.sum(jnp.where(rank2 < float(P_TOPK), 1.0, 0.0), axis=0, keepdims=True)
        bad = jnp.where(n1 + n2 != 2.0 * P_TOPK, 1.0, 0.0)
    atab = jnp.where(member, jnp.exp(s1 - v1_rows[0]) * (1.0 / z), 0.0)
    btab = jnp.exp(s2 - v2_rows[0])
    return ltab, atab, rank2, btab, bad


def _peer_prep(x_ref, mod_ref, n2g_ref, wqt_ref, ht_scr, qt_scr):
    tm = x_ref.shape[0]
    ncb = tm // LANES
    nq = wqt_ref.shape[0]
    x = x_ref[...]
    h = _rms(x) * n2g_ref[...] * (1.0 + mod_ref[4:5, :]) + mod_ref[3:4, :]
    htb = h.T.astype(BF16)
    ht_scr[...] = htb
    for half in range(2):
        rs = slice(half * (nq // 2), (half + 1) * (nq // 2))
        qt = _dot(wqt_ref[rs, :], htb)
        for cb in range(ncb):
            qt_scr[cb, rs, :] = qt[:, cb * LANES:(cb + 1) * LANES]


def _peer_head_block(hd, cb, keys_ref, qt_scr, tabs, exact):
    rank2_scr, b_scr, l_scr, a_scr = tabs
    ncb = qt_scr.shape[0]
    valid, flat = _cand_index(LANES)
    r1 = pl.multiple_of(hd * (2 * P_HALF), 2 * P_HALF)
    q1 = qt_scr[cb, pl.ds(r1, P_HALF), :].astype(BF16)
    q2 = qt_scr[cb, pl.ds(r1 + P_HALF, P_HALF), :].astype(BF16)
    s1 = _dot(keys_ref[2 * hd], q1)
    s2 = _dot(keys_ref[2 * hd + 1], q2)
    ltab, atab, rank2, btab, bad = _peer_head_tables(s1, s2, valid, flat, exact)
    n = hd * ncb + cb
    l_scr[n] = ltab
    a_scr[n] = atab
    rank2_scr[n] = rank2.astype(rank2_scr.dtype)
    b_scr[n] = btab.astype(b_scr.dtype)
    return bad


def _peer_head(hd, keys_ref, qt_scr, tabs, bad_scr, exact):
    bad = jnp.zeros((1, LANES), F32)
    for cb in range(qt_scr.shape[0]):
        bad_cb = _peer_head_block(hd, cb, keys_ref, qt_scr, tabs, exact)
        if not exact:
            bad = jnp.maximum(bad, bad_cb)
    if not exact:
        bad_scr[hd] = jnp.broadcast_to(bad, (SUBLANES, LANES))


def _peer_head_redo(hd, keys_ref, qt_scr, tabs, bad_scr):
    @pl.when(jnp.max(bad_scr[hd]) > 0.0)
    def _():
        _peer_head(hd, keys_ref, qt_scr, tabs, bad_scr, exact=True)


PEER_SUB = 512
MXU_COLS = 256
PEER_VMEM_LIMIT = 56 * 1024 * 1024


def _peer_kernel(x_ref, mod_ref, n2g_ref, wqt_ref, keys_ref, u_ref, vt_ref, o_ref,
                 ht_scr, qt_scr, rank2_scr, b_scr, l_scr, a_scr, acc_scr, bad_scr):
    j = pl.program_id(1)
    nj = pl.num_programs(1)
    tm = x_ref.shape[0]
    ncb = tm // LANES
    ec = u_ref.shape[0]
    rows_per_sub = PEER_SUB // N_KEYS
    tabs = (rank2_scr, b_scr, l_scr, a_scr)

    @pl.when(j == 0)
    def _():
        acc_scr[...] = jnp.zeros_like(acc_scr)
        _peer_prep(x_ref, mod_ref, n2g_ref, wqt_ref, ht_scr, qt_scr)

        def head_body(hd, carry):
            _peer_head(hd, keys_ref, qt_scr, tabs, bad_scr, exact=False)
            return carry

        def redo_body(hd, carry):
            _peer_head_redo(hd, keys_ref, qt_scr, tabs, bad_scr)
            return carry

        lax.fori_loop(0, P_HEADS, head_body, 0)
        lax.fori_loop(0, P_HEADS, redo_body, 0)

    n_half = 2
    eh = ec // n_half
    for nb in range(tm // MXU_COLS):
        for half in range(n_half):
            ns = slice(nb * MXU_COLS, (nb + 1) * MXU_COLS)
            act = _dot(u_ref[half * eh:(half + 1) * eh, :], ht_scr[:, ns])
            g_rows = []
            for q in range(eh // PEER_SUB):
                cols = []
                for c2 in range(MXU_COLS // LANES):
                    cb = nb * (MXU_COLS // LANES) + c2
                    ws = [jnp.zeros((N_KEYS, LANES), BF16) for _ in range(rows_per_sub)]
                    for hd in range(P_HEADS):
                        n = hd * ncb + cb
                        rank2 = rank2_scr[n]
                        btab = b_scr[n]
                        for r in range(rows_per_sub):
                            i1 = j * (ec // N_KEYS) + (half * eh + q * PEER_SUB) // N_KEYS + r
                            lrow = _bcast_row_bf16(l_scr[n, pl.ds(i1, 1), :])
                            arow = _bcast_row_bf16(a_scr[n, pl.ds(i1, 1), :])
                            ws[r] = ws[r] + jnp.where(rank2 < lrow, btab * arow, jnp.zeros_like(btab))
                    gs = []
                    for r in range(rows_per_sub):
                        r0 = q * PEER_SUB + r * N_KEYS
                        a_blk = act[r0:r0 + N_KEYS, c2 * LANES:(c2 + 1) * LANES]
                        gs.append(ws[r] * _gelu(a_blk.astype(BF16)))
                    cols.append(jnp.concatenate(gs, axis=0))
                g_rows.append(jnp.concatenate(cols, axis=1))
            g = jnp.concatenate(g_rows, axis=0)
            acc_scr[:, ns] += _dot(vt_ref[:, half * eh:(half + 1) * eh], g)

    @pl.when(j == nj - 1)
    def _():
        o_ref[...] = x_ref[...] + mod_ref[5:6, :] * acc_scr[...].T


def _peer(x, mod_l, row_of_tile, n2g, wqt, keys, u_tab, vt_tab, *, layer, tm, ec):
    t, d = x.shape
    nt = t // tm
    ne = u_tab.shape[1]
    ncb = tm // LANES
    nq = wqt.shape[1]
    tabs = lambda dt: pltpu.VMEM((P_HEADS * ncb, N_KEYS, LANES), dt)
    return pl.pallas_call(
        _peer_kernel,
        grid=(nt, ne // ec),
        in_specs=[pl.BlockSpec((tm, d), lambda i, j: (i, 0)),
                  pl.BlockSpec((None, 6, d), lambda i, j: (row_of_tile(i), 0, 0)),
                  pl.BlockSpec((1, d), lambda i, j: (0, 0)),
                  pl.BlockSpec((None, nq, d), lambda i, j: (layer, 0, 0), pipeline_mode=pl.Buffered(1)),
                  pl.BlockSpec((None, 2 * P_HEADS, N_KEYS, P_HALF), lambda i, j: (layer, 0, 0, 0),
                               pipeline_mode=pl.Buffered(1)),
                  pl.BlockSpec((None, ec, d), lambda i, j: (layer, j, 0)),
                  pl.BlockSpec((None, d, ec), lambda i, j: (layer, 0, j))],
        out_specs=pl.BlockSpec((tm, d), lambda i, j: (i, 0)),
        out_shape=jax.ShapeDtypeStruct((t, d), F32),
        scratch_shapes=[pltpu.VMEM((d, tm), BF16),
                        pltpu.VMEM((ncb, nq, LANES), F32),
                        tabs(BF16), tabs(BF16), tabs(F32), tabs(F32),
                        pltpu.VMEM((d, tm), F32),
                        pltpu.VMEM((P_HEADS, SUBLANES, LANES), F32)],
        compiler_params=_params(("parallel", "arbitrary"), PEER_VMEM_LIMIT),
        name="peer",
    )(x, mod_l, n2g, wqt, keys, u_tab, vt_tab)


def _rope_tables(n_tokens, dim):
    quarter = dim // 4
    pos = np.arange(n_tokens)
    row = (pos // GRID_W).astype(np.float32)
    col = (pos % GRID_W).astype(np.float32)
    lane = np.arange(LANES)
    dd = lane % dim
    half = dd // (dim // 2)
    e = dd % (dim // 2)
    second = e // quarter
    freq = e % quarter
    inv = jnp.asarray(ROPE_BASE, F32) ** (-jnp.arange(quarter, dtype=F32) / quarter)
    inv_lane = inv[freq]
    p = jnp.where(jnp.asarray(half == 0)[None, :], jnp.asarray(row)[:, None], jnp.asarray(col)[:, None])
    ang = p * inv_lane[None, :]
    cos, sin = jnp.cos(ang), jnp.sin(ang)
    first = jnp.asarray(second == 0)[None, :]
    return cos, jnp.where(first, -sin, 0.0), jnp.where(first, 0.0, sin)


def _block_diag_mean(width, group):
    idx = np.arange(width) // group
    return jnp.asarray((idx[:, None] == idx[None, :]).astype(np.float32) / group, BF16)


def kernel(x_prompt, x_sample, cache_a_k, cache_a_v, cache_b_k, cache_b_v, c, c_ctx, ada_w, ada_b, norm1_g, norm2_g, w_in, qn_a, kn_a, qn_b, kn_b, lam_q1, lam_k1, lam_q2, lam_k2, subln_g, out_norm_a, out_norm_c, conv_w, conv_b, w_out, peer_wq, peer_keys, peer_u, peer_v):
    nb, s_p, d = x_prompt.shape
    db, s_l, _ = x_sample.shape
    depth = w_in.shape[0]
    past = cache_a_k.shape[2]
    tm_p = s_p
    tm_l = min(MIX_TILE, s_l)
    tq_l = min(ATTN_Q_TILE, s_l)
    tq_p = min(ATTN_Q_TILE, s_p)
    assert s_l % tq_l == 0 and s_p % tq_p == 0
    tm_pre_p = min(MIX_TILE, nb * s_p)
    tm_peer = TOKEN_TILE
    ec = 2048
    assert s_l % tm_l == 0 and s_l % GRID_W == 0 and d == 1024 and (nb * s_p) % tm_pre_p == 0
    assert s_l % tm_peer == 0 and (nb * s_p) % tm_peer == 0 and tm_p % SUBLANES == 0

    n_cond = 1 + db
    n_rows = -(-n_cond // SUBLANES) * SUBLANES
    cond = jnp.concatenate([c_ctx[None, :], c, jnp.zeros((n_rows - n_cond, d), F32)], axis=0)
    mod = _modulation(cond, ada_w, ada_b).reshape(depth, n_rows, 6, d)

    def split_cols(w):
        offs = np.cumsum([0, 384, 128, 128, 384, 384, 384, 256, 256, 256])
        return [w[..., offs[k]:offs[k + 1]] for k in range(9)]

    def swap_kv(w):
        return jnp.concatenate([w[..., 64:128], w[..., 0:64]], axis=-1)

    wqa, wka, wva, wqb, wkb, wvb, wci, wcb, wcc = split_cols(w_in)
    win = jnp.concatenate([wqa, wka, swap_kv(wka), wva, swap_kv(wva), wqb, wkb, wvb, wci, wcb, wcc],
                          axis=-1).astype(BF16)
    wout = w_out.astype(BF16)
    wqt = jnp.swapaxes(peer_wq, 1, 2).astype(BF16)
    keys = peer_keys.reshape(depth, 2 * P_HEADS, N_KEYS, P_HALF).astype(BF16)
    u_tab = peer_u.astype(BF16)
    vt_tab = jnp.swapaxes(peer_v, 1, 2).astype(BF16)

    g64 = _block_diag_mean(384, HD)
    g32 = _block_diag_mean(384, B_DK)
    rope_a = _rope_tables(s_l, HD)
    rope_b = _rope_tables(s_l, B_DK)

    ck = cache_a_k.reshape(db, depth, past, 128)
    cv = cache_a_v.reshape(db, depth, past, 128)
    caches = (jnp.concatenate([ck, swap_kv(ck)], axis=-1).astype(BF16),
              jnp.concatenate([cv, swap_kv(cv)], axis=-1).astype(BF16),
              cache_b_k.reshape(db, depth, past, 384).astype(BF16),
              cache_b_v.reshape(db, depth, past, 384).astype(BF16))

    xp = x_prompt.reshape(nb * s_p, d)
    xs = x_sample.reshape(db * s_l, d)
    tps_p = s_p // tm_p
    tps_l = s_l // tm_l
    row_p = lambda i: 0
    row_l = lambda i: 1 + i // tps_l
    row_l_peer = lambda i: 1 + i // (s_l // tm_peer)

    new = [[], [], [], []]
    for l in range(depth):
        lam_init = 0.8 - 0.6 * math.exp(-0.3 * l)
        gains = ((jnp.tile(qn_a[l], A_HEADS) * HD ** -0.5)[None, :],
                 jnp.tile(kn_a[l], 2 * A_KV)[None, :],
                 (jnp.tile(qn_b[l], 2 * B_HEADS) * B_DK ** -0.5)[None, :],
                 jnp.tile(kn_b[l], 2 * B_HEADS)[None, :])
        lams = (lam_q1[l][None, :], lam_k1[l][None, :], lam_q2[l][None, :], lam_k2[l][None, :])
        gb = (jnp.tile(subln_g[l], B_HEADS) * (1.0 - lam_init))[None, :]
        post_args = (out_norm_a[l][None, :], gb, g64, out_norm_c[l][None, :], conv_w[l], conv_b[l][None, :],
                     wout)
        n1g = norm1_g[l][None, :]
        n2g = norm2_g[l][None, :]

        qa, ka2, va2, qb, kb, vb, u, cbg, cka, cva, ckb, cvb = _pre_mix(
            xp, mod[l], row_p, n1g, win, gains, (g64, g32), None, layer=l,
            tm=tm_pre_p, tiles_per_seq=1, emit_cache=True)
        new[0].append(cka.reshape(nb, s_p, A_KV, HD))
        new[1].append(cva.reshape(nb, s_p, A_KV, HD))
        new[2].append(ckb.reshape(nb, s_p, B_HEADS, 2, B_DK))
        new[3].append(cvb.reshape(nb, s_p, B_HEADS, 2 * B_DK))
        r3 = lambda a: a.reshape(nb, s_p, a.shape[-1])
        oa, ob = _attention(r3(qa), r3(ka2), r3(va2), r3(qb), r3(kb), r3(vb), None, lams,
                            layer=l, lam_init=lam_init, tq=tq_p)
        xp = _post_mix(oa.reshape(nb * s_p, 384), ob.reshape(nb * s_p, 384), u, cbg, xp, mod[l], row_p,
                       *post_args, layer=l, tm=tm_p, tiles_per_seq=tps_p)
        xp = _peer(xp, mod[l], row_p, n2g, wqt, keys, u_tab, vt_tab, layer=l, tm=tm_peer, ec=ec)

        qa, ka2, va2, qb, kb, vb, u, cbg = _pre_mix(
            xs, mod[l], row_l, n1g, win, gains, (g64, g32), rope_a + rope_b, layer=l,
            tm=tm_l, tiles_per_seq=tps_l, emit_cache=False)
        r3 = lambda a: a.reshape(db, s_l, a.shape[-1])
        oa, ob = _attention(r3(qa), r3(ka2), r3(va2), r3(qb), r3(kb), r3(vb), caches, lams,
                            layer=l, lam_init=lam_init, tq=tq_l)
        xs = _post_mix(oa.reshape(db * s_l, 384), ob.reshape(db * s_l, 384), u, cbg, xs, mod[l], row_l,
                       *post_args, layer=l, tm=tm_l, tiles_per_seq=tps_l)
        xs = _peer(xs, mod[l], row_l_peer, n2g, wqt, keys, u_tab, vt_tab, layer=l, tm=tm_peer, ec=ec)

    return (xp.reshape(nb, s_p, d), xs.reshape(db, s_l, d),
            jnp.stack(new[0], axis=1), jnp.stack(new[1], axis=1),
            jnp.stack(new[2], axis=1), jnp.stack(new[3], axis=1))
```

```python
import functools
import math

import numpy as np
import jax
import jax.numpy as jnp
from jax import lax
from jax.experimental import pallas as pl
from jax.experimental.pallas import tpu as pltpu

F32 = jnp.float32
BF16 = jnp.bfloat16

HD = 64
A_HEADS = 6
A_KV = 2
B_HEADS = 6
B_DK = 32
C_WIDTH = 256
A_WIDTH = A_HEADS * HD
B_WIDTH = B_HEADS * 2 * B_DK
GRID_W = 64
ROPE_BASE = 10000.0
EPS = 1e-6
P_HEADS = 8
N_KEYS = 128
P_HALF = 128
P_TOPK = 16

LANES = 128
SUBLANES = 8
VMEM_LIMIT = 48 * 1024 * 1024
TOKEN_TILE = 512
MIX_TILE = 1024
ATTN_Q_TILE = 128

_QA = (0, 384)
_KA2 = (384, 640)
_VA2 = (640, 896)
_QB = (896, 1280)
_KB = (1280, 1664)
_VB = (1664, 2048)
_CI = (2048, 2304)
_CB = (2304, 2560)
_CC = (2560, 2816)
NP_COLS = 2816

NEG_INF = float("-inf")


def _params(sem, vmem=VMEM_LIMIT):
    return pltpu.CompilerParams(dimension_semantics=sem, vmem_limit_bytes=vmem)


def _dot(a, b):
    return jnp.dot(a, b, preferred_element_type=F32)


def _dot_nt(a, b):
    return lax.dot_general(a, b, (((1,), (1,)), ((), ())), preferred_element_type=F32)


def _split3(a):
    hi = a.astype(BF16)
    r = a - hi.astype(F32)
    mid = r.astype(BF16)
    lo = (r - mid.astype(F32)).astype(BF16)
    return hi, mid, lo


def _dot_hp(a, b):
    a0, a1, a2 = _split3(a)
    b0, b1, b2 = _split3(b)
    return (_dot(a0, b0) + (_dot(a0, b1) + _dot(a1, b0))
            + (_dot(a0, b2) + _dot(a1, b1) + _dot(a2, b0)))


def _group_mean_sq(v, g_ref):
    v2 = v * v
    hi = v2.astype(BF16)
    lo = (v2 - hi.astype(F32)).astype(BF16)
    g = g_ref[...]
    return _dot(hi, g) + _dot(lo, g)


def _rms(x):
    return x * lax.rsqrt(jnp.mean(x * x, axis=-1, keepdims=True) + EPS)


def _mod_kernel(cond_ref, w_ref, b_ref, o_ref):
    c = cond_ref[...]
    s = c / (1.0 + jnp.exp(-c))
    o_ref[...] = _dot_hp(s, w_ref[...]) + b_ref[...]


def _modulation(cond, ada_w, ada_b):
    depth, d, d6 = ada_w.shape
    n = cond.shape[0]
    tn = 1536
    return pl.pallas_call(
        _mod_kernel,
        grid=(depth, d6 // tn),
        in_specs=[pl.BlockSpec((n, d), lambda l, j: (0, 0)),
                  pl.BlockSpec((None, d, tn), lambda l, j: (l, 0, j)),
                  pl.BlockSpec((None, 1, tn), lambda l, j: (l, 0, j))],
        out_specs=pl.BlockSpec((None, n, tn), lambda l, j: (l, 0, j)),
        out_shape=jax.ShapeDtypeStruct((depth, n, d6), F32),
        compiler_params=_params(("parallel", "parallel")),
        name="modulation",
    )(cond, ada_w, ada_b.reshape(depth, 1, d6))


def _rope_block(blk, c, sm, sp, shift):
    return (blk * c + pltpu.roll(blk, LANES - shift, 1) * sm + pltpu.roll(blk, shift, 1) * sp)


def _pre_mix_kernel(*refs, rope, emit_cache):
    it = iter(refs)
    x_ref, mod_ref, n1g_ref, win_ref = next(it), next(it), next(it), next(it)
    gqa_ref, gka_ref, gqb_ref, gkb_ref = next(it), next(it), next(it), next(it)
    g64_ref, g32_ref = next(it), next(it)
    if rope:
        ca_ref, sma_ref, spa_ref = next(it), next(it), next(it)
        cb_ref, smb_ref, spb_ref = next(it), next(it), next(it)
    qa_o, ka_o, va_o, qb_o, kb_o, vb_o, u_o, cbg_o = (next(it) for _ in range(8))
    if emit_cache:
        cka_o, cva_o, ckb_o, cvb_o = (next(it) for _ in range(4))

    x = x_ref[...]
    sh1 = mod_ref[0:1, :]
    sc1 = mod_ref[1:2, :]
    h = _rms(x) * n1g_ref[...] * (1.0 + sc1) + sh1
    hb = h.astype(BF16)

    def proj(seg):
        return _dot(hb, win_ref[:, seg[0]:seg[1]])

    def qk(seg, g_ref, gain_ref, width):
        p = proj(seg)
        ms = _group_mean_sq(p, g_ref) if width == 384 else _dot_group256(p, g_ref)
        return p * lax.rsqrt(ms + EPS), gain_ref[...]

    def _dot_group256(p, g_ref):
        v2 = p * p
        hi = v2.astype(BF16)
        lo = (v2 - hi.astype(F32)).astype(BF16)
        g = g_ref[0:256, 0:256]
        return _dot(hi, g) + _dot(lo, g)

    def finish(y, gain, out_ref, tabs, shift, cache_ref=None, cache_cols=None):
        nblk = y.shape[1] // LANES
        for m in range(nblk):
            sl = slice(m * LANES, (m + 1) * LANES)
            blk = y[:, sl] * gain[:, sl]
            if cache_ref is not None and m < cache_cols // LANES:
                cache_ref[:, sl] = blk
            if tabs is not None:
                blk = _rope_block(blk, tabs[0][...], tabs[1][...], tabs[2][...], shift)
            out_ref[:, sl] = blk.astype(out_ref.dtype)

    tabs_a = (ca_ref, sma_ref, spa_ref) if rope else None
    tabs_b = (cb_ref, smb_ref, spb_ref) if rope else None

    y, g = qk(_QA, g64_ref, gqa_ref, 384)
    finish(y, g, qa_o, tabs_a, HD // 4)
    y, g = qk(_KA2, g64_ref, gka_ref, 256)
    finish(y, g, ka_o, tabs_a, HD // 4, cka_o if emit_cache else None, 128)
    va = proj(_VA2)
    va_o[...] = va.astype(va_o.dtype)
    if emit_cache:
        cva_o[...] = va[:, 0:128]
    y, g = qk(_QB, g32_ref, gqb_ref, 384)
    finish(y, g, qb_o, tabs_b, B_DK // 4)
    y, g = qk(_KB, g32_ref, gkb_ref, 384)
    finish(y, g, kb_o, tabs_b, B_DK // 4, ckb_o if emit_cache else None, 384)
    vb = proj(_VB)
    vb_o[...] = vb.astype(vb_o.dtype)
    if emit_cache:
        cvb_o[...] = vb
    u_o[...] = proj(_CC) * proj(_CI)
    cbg_o[...] = proj(_CB)


def _pre_mix(x, mod_l, row_of_tile, n1g, win, gains, gmats, rope_tabs, *, layer, tm, tiles_per_seq, emit_cache):
    t, d = x.shape
    nt = t // tm
    rope = rope_tabs is not None
    const = lambda shape: pl.BlockSpec(shape, lambda i: (0,) * len(shape))
    in_specs = [pl.BlockSpec((tm, d), lambda i: (i, 0)),
                pl.BlockSpec((None, 6, d), lambda i: (row_of_tile(i), 0, 0)),
                const((1, d)), pl.BlockSpec((None, d, NP_COLS), lambda i: (layer, 0, 0)),
                const((1, 384)), const((1, 256)), const((1, 384)), const((1, 384)),
                const((384, 384)), const((384, 384))]
    args = [x, mod_l, n1g, win, *gains, *gmats]
    if rope:
        in_specs += [pl.BlockSpec((tm, LANES), lambda i: (i % tiles_per_seq, 0))] * 6
        args += list(rope_tabs)
    widths = [(384, BF16), (256, BF16), (256, BF16), (384, BF16), (384, BF16), (384, BF16),
              (256, F32), (256, F32)]
    if emit_cache:
        widths += [(128, F32), (128, F32), (384, F32), (384, F32)]
    out_shape = [jax.ShapeDtypeStruct((t, w), dt) for w, dt in widths]
    out_specs = [pl.BlockSpec((tm, w), lambda i: (i, 0)) for w, _ in widths]
    return pl.pallas_call(
        functools.partial(_pre_mix_kernel, rope=rope, emit_cache=emit_cache),
        grid=(nt,), in_specs=in_specs, out_specs=out_specs, out_shape=out_shape,
        compiler_params=_params(("parallel",)),
        name="pre_mix_rope" if rope else "pre_mix",
    )(*args)


def _softmax_pv(q, k_parts, v_parts):
    ss = [_dot_nt(q, k) for k in k_parts]
    m = jnp.max(ss[0], axis=-1, keepdims=True)
    for s in ss[1:]:
        m = jnp.maximum(m, jnp.max(s, axis=-1, keepdims=True))
    l = None
    o = None
    for s, v in zip(ss, v_parts):
        p = jnp.exp(s - m)
        ls = jnp.sum(p, axis=-1, keepdims=True)
        os_ = _dot(p.astype(BF16), v)
        l = ls if l is None else l + ls
        o = os_ if o is None else o + os_
    return o * (1.0 / l)


def _attn_kernel(*refs, has_cache, lam_init):
    it = iter(refs)
    qa_ref, ka_ref, va_ref, qb_ref, kb_ref, vb_ref = (next(it) for _ in range(6))
    if has_cache:
        cka_ref, cva_ref, ckb_ref, cvb_ref = (next(it) for _ in range(4))
    lq1_ref, lk1_ref, lq2_ref, lk2_ref = (next(it) for _ in range(4))
    oa_ref, ob_ref = next(it), next(it)

    tq = qa_ref.shape[0]
    lane = lax.broadcasted_iota(jnp.int32, (tq, LANES), 1)
    lo_half = lane < 64

    lam = (jnp.exp(jnp.sum(lq1_ref[...] * lk1_ref[...], axis=-1, keepdims=True))
           - jnp.exp(jnp.sum(lq2_ref[...] * lk2_ref[...], axis=-1, keepdims=True)) + lam_init)

    o_a = [None] * A_HEADS
    for kblk in range(2):
        heads = [h for h in range(A_HEADS) if (0 if h // (A_HEADS // A_KV) == h % 2 else 1) == kblk]
        ksl = slice(kblk * LANES, (kblk + 1) * LANES)
        qs = []
        for h in heads:
            qblk = qa_ref[:, (h // 2) * LANES:(h // 2 + 1) * LANES]
            keep = lo_half if h % 2 == 0 else jnp.logical_not(lo_half)
            qs.append(jnp.where(keep, qblk, jnp.zeros_like(qblk)))
        ks, vs = [ka_ref[:, ksl]], [va_ref[:, ksl]]
        if has_cache:
            ks.append(cka_ref[:, ksl])
            vs.append(cva_ref[:, ksl])
        o = _softmax_pv(jnp.concatenate(qs, axis=0), ks, vs)
        for k, h in enumerate(heads):
            o_a[h] = o[k * tq:(k + 1) * tq, :]
    for m in range(A_HEADS // 2):
        oa_ref[:, m * LANES:(m + 1) * LANES] = jnp.where(lo_half, o_a[2 * m], o_a[2 * m + 1])

    seg_id = lane >> 5
    for m in range(B_HEADS // 2):
        sl = slice(m * LANES, (m + 1) * LANES)
        qblk = qb_ref[:, sl]
        ks, vs = [kb_ref[:, sl]], [vb_ref[:, sl]]
        if has_cache:
            ks.append(ckb_ref[:, sl])
            vs.append(cvb_ref[:, sl])
        qs = [jnp.where(seg_id == seg, qblk, jnp.zeros_like(qblk)) for seg in range(4)]
        o = _softmax_pv(jnp.concatenate(qs, axis=0), ks, vs)
        outs = [o[(2 * par) * tq:(2 * par + 1) * tq, :] - lam * o[(2 * par + 1) * tq:(2 * par + 2) * tq, :]
                for par in range(2)]
        ob_ref[:, sl] = jnp.where(lo_half, outs[0], outs[1])


def _attention(qa, ka2, va2, qb, kb, vb, caches, lams, *, layer, lam_init, tq):
    b, s, _ = qa.shape
    has_cache = caches is not None
    qspec = lambda w: pl.BlockSpec((None, tq, w), lambda i, j: (i, j, 0))
    kspec = lambda w: pl.BlockSpec((None, s, w), lambda i, j: (i, 0, 0))
    in_specs = [qspec(384), kspec(256), kspec(256), qspec(384), kspec(384), kspec(384)]
    args = [qa, ka2, va2, qb, kb, vb]
    if has_cache:
        lc = caches[0].shape[2]
        cspec = lambda w: pl.BlockSpec((None, None, lc, w), lambda i, j: (i, layer, 0, 0))
        in_specs += [cspec(256), cspec(256), cspec(384), cspec(384)]
        args += list(caches)
    in_specs += [pl.BlockSpec((1, B_DK), lambda i, j: (0, 0))] * 4
    args += list(lams)
    return pl.pallas_call(
        functools.partial(_attn_kernel, has_cache=has_cache, lam_init=lam_init),
        grid=(b, s // tq), in_specs=in_specs,
        out_specs=[qspec(384), qspec(384)],
        out_shape=[jax.ShapeDtypeStruct((b, s, 384), F32)] * 2,
        compiler_params=_params(("parallel", "parallel")),
        name="attention_cached" if has_cache else "attention",
    )(*args)


def _post_mix_kernel(oa_ref, ob_ref, u_ref, up_ref, un_ref, cbg_ref, x_ref, mod_ref,
                     ga_ref, gb_ref, g64_ref, gc_ref, cw_ref, cbias_ref, wout_ref, o_ref,
                     *, tiles_per_seq):
    i = pl.program_id(0)
    tm = x_ref.shape[0]
    pos = i % tiles_per_seq
    u = u_ref[...]
    row = lax.broadcasted_iota(jnp.int32, u.shape, 0)
    prev_row = jnp.where(pos == 0, 0.0, 1.0) * up_ref[SUBLANES - 1:SUBLANES, :]
    next_row = jnp.where(pos == tiles_per_seq - 1, 0.0, 1.0) * un_ref[0:1, :]
    u_dn = jnp.where(row == 0, prev_row, pltpu.roll(u, 1, 0))
    u_up = jnp.where(row == tm - 1, next_row, pltpu.roll(u, tm - 1, 0))
    y = u_dn * cw_ref[0:1, :] + u * cw_ref[1:2, :] + u_up * cw_ref[2:3, :] + cbias_ref[...]
    oc = cbg_ref[...] * y

    na = _rms(oa_ref[...]) * ga_ref[...]
    ob = ob_ref[...]
    nb = ob * lax.rsqrt(_group_mean_sq(ob, g64_ref) + EPS) * gb_ref[...]
    nc = _rms(oc) * gc_ref[...]
    mix = (_dot(na.astype(BF16), wout_ref[0:384, :]) + _dot(nb.astype(BF16), wout_ref[384:768, :])
           + _dot(nc.astype(BF16), wout_ref[768:1024, :]))
    o_ref[...] = x_ref[...] + mod_ref[2:3, :] * mix


def _post_mix(oa, ob, u, cbg, x, mod_l, row_of_tile, ga, gb, g64, gc, cw, cbias, wout, *, layer, tm, tiles_per_seq):
    t, d = x.shape
    nt = t // tm
    nb8 = t // SUBLANES
    step = tm // SUBLANES
    const = lambda shape: pl.BlockSpec(shape, lambda i: (0,) * len(shape))
    tok = lambda w: pl.BlockSpec((tm, w), lambda i: (i, 0))
    in_specs = [tok(384), tok(384), tok(256),
                pl.BlockSpec((SUBLANES, 256), lambda i: (jnp.maximum(i * step - 1, 0), 0)),
                pl.BlockSpec((SUBLANES, 256), lambda i: (jnp.minimum((i + 1) * step, nb8 - 1), 0)),
                tok(256), tok(d),
                pl.BlockSpec((None, 6, d), lambda i: (row_of_tile(i), 0, 0)),
                const((1, 384)), const((1, 384)), const((384, 384)), const((1, 256)),
                const((3, 256)), const((1, 256)), pl.BlockSpec((None, d, d), lambda i: (layer, 0, 0))]
    return pl.pallas_call(
        functools.partial(_post_mix_kernel, tiles_per_seq=tiles_per_seq),
        grid=(nt,), in_specs=in_specs, out_specs=tok(d),
        out_shape=jax.ShapeDtypeStruct((t, d), F32),
        compiler_params=_params(("parallel",)),
        name="post_mix",
    )(oa, ob, u, u, u, cbg, x, mod_l, ga, gb, g64, gc, cw, cbias, wout)


def _gelu(x):
    k1 = -2.0 * math.sqrt(2.0 / math.pi) * math.log2(math.e)
    return x / (1.0 + jnp.exp2(x * (k1 + (k1 * 0.044715) * (x * x))))


BF16_SUBLANES = 16


def _bcast_row_bf16(row):
    packed = jnp.broadcast_to(row, (BF16_SUBLANES, LANES)).astype(BF16)
    return jnp.concatenate([packed] * (N_KEYS // BF16_SUBLANES), axis=0)


def _top16(s):
    n = s.shape[0]
    rows = lax.broadcasted_iota(jnp.int32, s.shape, 0).astype(F32)
    work = s
    rank = jnp.full(s.shape, float(P_TOPK), F32)
    vals = []
    for a in range(P_TOPK):
        m = jnp.max(work, axis=0, keepdims=True)
        idx = jnp.min(jnp.where(work == m, rows, float(n)), axis=0, keepdims=True)
        hit = rows == idx
        rank = jnp.where(hit, float(a), rank)
        work = jnp.where(hit, NEG_INF, work)
        vals.append(m)
    return vals, rank


def _top16_distinct_pair(s1, s2):
    work1, work2 = s1, s2
    rank2 = jnp.full(s2.shape, float(P_TOPK), F32)
    vals1, vals2 = [], []
    for a in range(P_TOPK):
        m1 = jnp.max(work1, axis=0, keepdims=True)
        m2 = jnp.max(work2, axis=0, keepdims=True)
        hit2 = work2 == m2
        work1 = jnp.where(work1 == m1, NEG_INF, work1)
        rank2 = jnp.where(hit2, float(a), rank2)
        work2 = jnp.where(hit2, NEG_INF, work2)
        vals1.append(m1)
        vals2.append(m2)
    return vals1, vals2, rank2


def _stack16(rows_list):
    w = rows_list[0].shape[1]
    r = lax.broadcasted_iota(jnp.int32, (P_TOPK, w), 0)
    out = jnp.zeros((P_TOPK, w), F32)
    for a, v in enumerate(rows_list):
        out = jnp.where(r == a, v, out)
    return out


_CAND_ROWS = 80


def _cand_build(v1_rows, v1_arr, v2_arr, v2_row0, op):
    pieces = [op(v1_rows[0], v2_arr), op(v1_rows[1], v2_arr[0:8])]
    for a in range(2, 8):
        pieces.append(op(v1_rows[a], v2_arr[0:8]))
    pieces.append(op(v1_arr[8:16], v2_row0))
    return jnp.concatenate(pieces, axis=0)


def _cand_index(w):
    row = lax.broadcasted_iota(jnp.int32, (_CAND_ROWS, w), 0)
    a = jnp.where(row < 16, 0, jnp.where(row < 72, 1 + ((row - 16) >> 3), 8 + (row - 72)))
    b = jnp.where(row < 16, row, jnp.where(row < 72, (row - 16) & 7, 0))
    valid = (a + 1) * (b + 1) <= P_TOPK
    flat = (a * P_TOPK + b).astype(F32)
    return valid, flat


def _peer_head_tables(s1, s2, valid, flat, exact):
    if exact:
        v1_rows, rank1 = _top16(s1)
        v2_rows, rank2 = _top16(s2)
    else:
        v1_rows, v2_rows, rank2 = _top16_distinct_pair(s1, s2)
    v1 = _stack16(v1_rows)
    v2 = _stack16(v2_rows)
    e1 = jnp.exp(v1 - v1_rows[0])
    e2 = jnp.exp(v2 - v2_rows[0])
    e1_rows = [e1[a:a + 1, :] for a in range(8)]
    cand = _cand_build(v1_rows, v1, v2, v2_rows[0], lambda p, q: p + q)
    gcand = _cand_build(e1_rows, e1, e2, e2[0:1, :], lambda p, q: p * q)
    work = jnp.where(valid, cand, NEG_INF)
    sel = jnp.zeros_like(cand)
    for _ in range(P_TOPK):
        m = jnp.max(work, axis=0, keepdims=True)
        idx = jnp.min(jnp.where(work == m, flat, 1e9), axis=0, keepdims=True)
        hit = flat == idx
        sel = jnp.where(hit, 1.0, sel)
        work = jnp.where(hit, NEG_INF, work)
    z = jnp.sum(sel * gcand, axis=0, keepdims=True)
    counts = [jnp.sum(sel[0:16], axis=0, keepdims=True)]
    for a in range(1, 8):
        counts.append(jnp.sum(sel[8 + 8 * a:16 + 8 * a], axis=0, keepdims=True))
    for a in range(8, 16):
        counts.append(sel[64 + a:65 + a, :])
    ltab = jnp.zeros_like(s1)
    if exact:
        member = rank1 < float(P_TOPK)
        for a in range(P_TOPK):
            ltab = jnp.where(rank1 == float(a), counts[a], ltab)
        bad = None
    else:
        member = s1 >= v1_rows[P_TOPK - 1]
        for a in range(P_TOPK):
            ltab = jnp.where(s1 == v1_rows[a], counts[a], ltab)
        n1 = jnp.sum(jnp.where(member, 1.0, 0.0), axis=0, keepdims=True)
        n2 = jnp.sum(jnp.where(rank2 < float(P_TOPK), 1.0, 0.0), axis=0, keepdims=True)
        bad = jnp.where(n1 + n2 != 2.0 * P_TOPK, 1.0, 0.0)
    atab = jnp.where(member, jnp.exp(s1 - v1_rows[0]) * (1.0 / z), 0.0)
    btab = jnp.exp(s2 - v2_rows[0])
    return ltab, atab, rank2, btab, bad


def _peer_prep(x_ref, mod_ref, n2g_ref, wqt_ref, ht_scr, qt_scr):
    tm = x_ref.shape[0]
    ncb = tm // LANES
    nq = wqt_ref.shape[0]
    x = x_ref[...]
    h = _rms(x) * n2g_ref[...] * (1.0 + mod_ref[4:5, :]) + mod_ref[3:4, :]
    htb = h.T.astype(BF16)
    ht_scr[...] = htb
    for half in range(2):
        rs = slice(half * (nq // 2), (half + 1) * (nq // 2))
        qt = _dot(wqt_ref[rs, :], htb)
        for cb in range(ncb):
            qt_scr[cb, rs, :] = qt[:, cb * LANES:(cb + 1) * LANES]


def _peer_head_block(hd, cb, keys_ref, qt_scr, tabs, exact):
    rank2_scr, b_scr, l_scr, a_scr = tabs
    ncb = qt_scr.shape[0]
    valid, flat = _cand_index(LANES)
    r1 = pl.multiple_of(hd * (2 * P_HALF), 2 * P_HALF)
    q1 = qt_scr[cb, pl.ds(r1, P_HALF), :].astype(BF16)
    q2 = qt_scr[cb, pl.ds(r1 + P_HALF, P_HALF), :].astype(BF16)
    s1 = _dot(keys_ref[2 * hd], q1)
    s2 = _dot(keys_ref[2 * hd + 1], q2)
    ltab, atab, rank2, btab, bad = _peer_head_tables(s1, s2, valid, flat, exact)
    n = hd * ncb + cb
    l_scr[n] = ltab
    a_scr[n] = atab
    rank2_scr[n] = rank2.astype(rank2_scr.dtype)
    b_scr[n] = btab.astype(b_scr.dtype)
    return bad


def _peer_head(hd, keys_ref, qt_scr, tabs, bad_scr, exact):
    bad = jnp.zeros((1, LANES), F32)
    for cb in range(qt_scr.shape[0]):
        bad_cb = _peer_head_block(hd, cb, keys_ref, qt_scr, tabs, exact)
        if not exact:
            bad = jnp.maximum(bad, bad_cb)
    if not exact:
        bad_scr[hd] = jnp.broadcast_to(bad, (SUBLANES, LANES))


def _peer_head_redo(hd, keys_ref, qt_scr, tabs, bad_scr):
    @pl.when(jnp.max(bad_scr[hd]) > 0.0)
    def _():
        _peer_head(hd, keys_ref, qt_scr, tabs, bad_scr, exact=True)


PEER_SUB = 512
MXU_COLS = 256
PEER_VMEM_LIMIT = 56 * 1024 * 1024


def _peer_kernel(x_ref, mod_ref, n2g_ref, wqt_ref, keys_ref, u_ref, vt_ref, o_ref,
                 ht_scr, qt_scr, rank2_scr, b_scr, l_scr, a_scr, acc_scr, bad_scr):
    j = pl.program_id(1)
    nj = pl.num_programs(1)
    tm = x_ref.shape[0]
    ncb = tm // LANES
    ec = u_ref.shape[0]
    rows_per_sub = PEER_SUB // N_KEYS
    tabs = (rank2_scr, b_scr, l_scr, a_scr)

    @pl.when(j == 0)
    def _():
        acc_scr[...] = jnp.zeros_like(acc_scr)
        _peer_prep(x_ref, mod_ref, n2g_ref, wqt_ref, ht_scr, qt_scr)

        def head_body(hd, carry):
            _peer_head(hd, keys_ref, qt_scr, tabs, bad_scr, exact=False)
            return carry

        def redo_body(hd, carry):
            _peer_head_redo(hd, keys_ref, qt_scr, tabs, bad_scr)
            return carry

        lax.fori_loop(0, P_HEADS, head_body, 0)
        lax.fori_loop(0, P_HEADS, redo_body, 0)

    n_half = 2
    eh = ec // n_half
    for nb in range(tm // MXU_COLS):
        for half in range(n_half):
            ns = slice(nb * MXU_COLS, (nb + 1) * MXU_COLS)
            act = _dot(u_ref[half * eh:(half + 1) * eh, :], ht_scr[:, ns])
            g_rows = []
            for q in range(eh // PEER_SUB):
                cols = []
                for c2 in range(MXU_COLS // LANES):
                    cb = nb * (MXU_COLS // LANES) + c2
                    ws = [jnp.zeros((N_KEYS, LANES), BF16) for _ in range(rows_per_sub)]
                    for hd in range(P_HEADS):
                        n = hd * ncb + cb
                        rank2 = rank2_scr[n]
                        btab = b_scr[n]
                        for r in range(rows_per_sub):
                            i1 = j * (ec // N_KEYS) + (half * eh + q * PEER_SUB) // N_KEYS + r
                            lrow = _bcast_row_bf16(l_scr[n, pl.ds(i1, 1), :])
                            arow = _bcast_row_bf16(a_scr[n, pl.ds(i1, 1), :])
                            ws[r] = ws[r] + jnp.where(rank2 < lrow, btab * arow, jnp.zeros_like(btab))
                    gs = []
                    for r in range(rows_per_sub):
                        r0 = q * PEER_SUB + r * N_KEYS
                        a_blk = act[r0:r0 + N_KEYS, c2 * LANES:(c2 + 1) * LANES]
                        gs.append(ws[r] * _gelu(a_blk.astype(BF16)))
                    cols.append(jnp.concatenate(gs, axis=0))
                g_rows.append(jnp.concatenate(cols, axis=1))
            g = jnp.concatenate(g_rows, axis=0)
            acc_scr[:, ns] += _dot(vt_ref[:, half * eh:(half + 1) * eh], g)

    @pl.when(j == nj - 1)
    def _():
        o_ref[...] = x_ref[...] + mod_ref[5:6, :] * acc_scr[...].T


def _peer(x, mod_l, row_of_tile, n2g, wqt, keys, u_tab, vt_tab, *, layer, tm, ec):
    t, d = x.shape
    nt = t // tm
    ne = u_tab.shape[1]
    ncb = tm // LANES
    nq = wqt.shape[1]
    tabs = lambda dt: pltpu.VMEM((P_HEADS * ncb, N_KEYS, LANES), dt)
    return pl.pallas_call(
        _peer_kernel,
        grid=(nt, ne // ec),
        in_specs=[pl.BlockSpec((tm, d), lambda i, j: (i, 0)),
                  pl.BlockSpec((None, 6, d), lambda i, j: (row_of_tile(i), 0, 0)),
                  pl.BlockSpec((1, d), lambda i, j: (0, 0)),
                  pl.BlockSpec((None, nq, d), lambda i, j: (layer, 0, 0), pipeline_mode=pl.Buffered(1)),
                  pl.BlockSpec((None, 2 * P_HEADS, N_KEYS, P_HALF), lambda i, j: (layer, 0, 0, 0),
                               pipeline_mode=pl.Buffered(1)),
                  pl.BlockSpec((None, ec, d), lambda i, j: (layer, j, 0)),
                  pl.BlockSpec((None, d, ec), lambda i, j: (layer, 0, j))],
        out_specs=pl.BlockSpec((tm, d), lambda i, j: (i, 0)),
        out_shape=jax.ShapeDtypeStruct((t, d), F32),
        scratch_shapes=[pltpu.VMEM((d, tm), BF16),
                        pltpu.VMEM((ncb, nq, LANES), F32),
                        tabs(BF16), tabs(BF16), tabs(F32), tabs(F32),
                        pltpu.VMEM((d, tm), F32),
                        pltpu.VMEM((P_HEADS, SUBLANES, LANES), F32)],
        compiler_params=_params(("parallel", "arbitrary"), PEER_VMEM_LIMIT),
        name="peer",
    )(x, mod_l, n2g, wqt, keys, u_tab, vt_tab)


def _route_kernel(x_ref, mod_ref, n2g_ref, wqt_ref, keys_ref,
                  ht_ref, rank2_ref, b_ref, l_ref, a_ref, qt_scr, bad_scr):
    tabs = (rank2_ref, b_ref, l_ref, a_ref)
    _peer_prep(x_ref, mod_ref, n2g_ref, wqt_ref, ht_ref, qt_scr)

    def head_body(hd, carry):
        _peer_head(hd, keys_ref, qt_scr, tabs, bad_scr, exact=False)
        return carry

    def redo_body(hd, carry):
        _peer_head_redo(hd, keys_ref, qt_scr, tabs, bad_scr)
        return carry

    lax.fori_loop(0, P_HEADS, head_body, 0)
    lax.fori_loop(0, P_HEADS, redo_body, 0)


def _experts_kernel(x_ref, mod_ref, ht_scr, rank2_scr, b_scr, l_scr, a_scr, u_ref, vt_ref, o_ref, acc_scr):
    j = pl.program_id(1)
    nj = pl.num_programs(1)
    tm = x_ref.shape[0]
    ncb = tm // LANES
    ec = u_ref.shape[0]
    rows_per_sub = PEER_SUB // N_KEYS

    @pl.when(j == 0)
    def _():
        acc_scr[...] = jnp.zeros_like(acc_scr)

    n_half = 2
    eh = ec // n_half
    for nb in range(tm // MXU_COLS):
        for half in range(n_half):
            ns = slice(nb * MXU_COLS, (nb + 1) * MXU_COLS)
            act = _dot(u_ref[half * eh:(half + 1) * eh, :], ht_scr[:, ns])
            g_rows = []
            for q in range(eh // PEER_SUB):
                cols = []
                for c2 in range(MXU_COLS // LANES):
                    cb = nb * (MXU_COLS // LANES) + c2
                    ws = [jnp.zeros((N_KEYS, LANES), BF16) for _ in range(rows_per_sub)]
                    for hd in range(P_HEADS):
                        n = hd * ncb + cb
                        rank2 = rank2_scr[n]
                        btab = b_scr[n]
                        for r in range(rows_per_sub):
                            i1 = j * (ec // N_KEYS) + (half * eh + q * PEER_SUB) // N_KEYS + r
                            lrow = _bcast_row_bf16(l_scr[n, pl.ds(i1, 1), :])
                            arow = _bcast_row_bf16(a_scr[n, pl.ds(i1, 1), :])
                            ws[r] = ws[r] + jnp.where(rank2 < lrow, btab * arow, jnp.zeros_like(btab))
                    gs = []
                    for r in range(rows_per_sub):
                        r0 = q * PEER_SUB + r * N_KEYS
                        a_blk = act[r0:r0 + N_KEYS, c2 * LANES:(c2 + 1) * LANES]
                        gs.append(ws[r] * _gelu(a_blk.astype(BF16)))
                    cols.append(jnp.concatenate(gs, axis=0))
                g_rows.append(jnp.concatenate(cols, axis=1))
            g = jnp.concatenate(g_rows, axis=0)
            acc_scr[:, ns] += _dot(vt_ref[:, half * eh:(half + 1) * eh], g)

    @pl.when(j == nj - 1)
    def _():
        o_ref[...] = x_ref[...] + mod_ref[5:6, :] * acc_scr[...].T


def _peer_split(x, mod_l, row_of_tile, n2g, wqt, keys, u_tab, vt_tab, *, layer, tm, ec):
    t, d = x.shape
    nt = t // tm
    ne = u_tab.shape[1]
    ncb = tm // LANES
    nq = wqt.shape[1]
    n_tab = P_HEADS * ncb
    tab_shape = lambda dt: jax.ShapeDtypeStruct((nt * n_tab, N_KEYS, LANES), dt)
    tab_spec1 = pl.BlockSpec((n_tab, N_KEYS, LANES), lambda i: (i, 0, 0))
    ht, rank2, btab, ltab, atab = pl.pallas_call(
        _route_kernel,
        grid=(nt,),
        in_specs=[pl.BlockSpec((tm, d), lambda i: (i, 0)),
                  pl.BlockSpec((None, 6, d), lambda i: (row_of_tile(i), 0, 0)),
                  pl.BlockSpec((1, d), lambda i: (0, 0)),
                  pl.BlockSpec((None, nq, d), lambda i: (layer, 0, 0)),
                  pl.BlockSpec((None, 2 * P_HEADS, N_KEYS, P_HALF), lambda i: (layer, 0, 0, 0))],
        out_specs=[pl.BlockSpec((d, tm), lambda i: (0, i)), tab_spec1, tab_spec1, tab_spec1, tab_spec1],
        out_shape=[jax.ShapeDtypeStruct((d, t), BF16), tab_shape(BF16), tab_shape(BF16),
                   tab_shape(F32), tab_shape(F32)],
        scratch_shapes=[pltpu.VMEM((ncb, nq, LANES), F32),
                        pltpu.VMEM((P_HEADS, SUBLANES, LANES), F32)],
        compiler_params=_params(("parallel",)),
        name="peer_route",
    )(x, mod_l, n2g, wqt, keys)
    tab_spec2 = pl.BlockSpec((n_tab, N_KEYS, LANES), lambda i, j: (i, 0, 0))
    return pl.pallas_call(
        _experts_kernel,
        grid=(nt, ne // ec),
        in_specs=[pl.BlockSpec((tm, d), lambda i, j: (i, 0)),
                  pl.BlockSpec((None, 6, d), lambda i, j: (row_of_tile(i), 0, 0)),
                  pl.BlockSpec((d, tm), lambda i, j: (0, i)),
                  tab_spec2, tab_spec2, tab_spec2, tab_spec2,
                  pl.BlockSpec((None, ec, d), lambda i, j: (layer, j, 0)),
                  pl.BlockSpec((None, d, ec), lambda i, j: (layer, 0, j))],
        out_specs=pl.BlockSpec((tm, d), lambda i, j: (i, 0)),
        out_shape=jax.ShapeDtypeStruct((t, d), F32),
        scratch_shapes=[pltpu.VMEM((d, tm), F32)],
        compiler_params=_params(("parallel", "arbitrary"), PEER_VMEM_LIMIT),
        name="peer_experts",
    )(x, mod_l, ht, rank2, btab, ltab, atab, u_tab, vt_tab)


def _rope_tables(n_tokens, dim):
    quarter = dim // 4
    pos = np.arange(n_tokens)
    row = (pos // GRID_W).astype(np.float32)
    col = (pos % GRID_W).astype(np.float32)
    lane = np.arange(LANES)
    dd = lane % dim
    half = dd // (dim // 2)
    e = dd % (dim // 2)
    second = e // quarter
    freq = e % quarter
    inv = jnp.asarray(ROPE_BASE, F32) ** (-jnp.arange(quarter, dtype=F32) / quarter)
    inv_lane = inv[freq]
    p = jnp.where(jnp.asarray(half == 0)[None, :], jnp.asarray(row)[:, None], jnp.asarray(col)[:, None])
    ang = p * inv_lane[None, :]
    cos, sin = jnp.cos(ang), jnp.sin(ang)
    first = jnp.asarray(second == 0)[None, :]
    return cos, jnp.where(first, -sin, 0.0), jnp.where(first, 0.0, sin)


def _block_diag_mean(width, group):
    idx = np.arange(width) // group
    return jnp.asarray((idx[:, None] == idx[None, :]).astype(np.float32) / group, BF16)


def kernel(x_prompt, x_sample, cache_a_k, cache_a_v, cache_b_k, cache_b_v, c, c_ctx, ada_w, ada_b, norm1_g, norm2_g, w_in, qn_a, kn_a, qn_b, kn_b, lam_q1, lam_k1, lam_q2, lam_k2, subln_g, out_norm_a, out_norm_c, conv_w, conv_b, w_out, peer_wq, peer_keys, peer_u, peer_v):
    nb, s_p, d = x_prompt.shape
    db, s_l, _ = x_sample.shape
    depth = w_in.shape[0]
    past = cache_a_k.shape[2]
    tm_p = s_p
    tm_l = min(MIX_TILE, s_l)
    tq_l = min(ATTN_Q_TILE, s_l)
    tq_p = min(ATTN_Q_TILE, s_p)
    assert s_l % tq_l == 0 and s_p % tq_p == 0
    tm_pre_p = min(MIX_TILE, nb * s_p)
    tm_peer = TOKEN_TILE
    ec = 2048
    assert s_l % tm_l == 0 and s_l % GRID_W == 0 and d == 1024 and (nb * s_p) % tm_pre_p == 0
    assert s_l % tm_peer == 0 and (nb * s_p) % tm_peer == 0 and tm_p % SUBLANES == 0

    n_cond = 1 + db
    n_rows = -(-n_cond // SUBLANES) * SUBLANES
    cond = jnp.concatenate([c_ctx[None, :], c, jnp.zeros((n_rows - n_cond, d), F32)], axis=0)
    mod = _modulation(cond, ada_w, ada_b).reshape(depth, n_rows, 6, d)

    def split_cols(w):
        offs = np.cumsum([0, 384, 128, 128, 384, 384, 384, 256, 256, 256])
        return [w[..., offs[k]:offs[k + 1]] for k in range(9)]

    def swap_kv(w):
        return jnp.concatenate([w[..., 64:128], w[..., 0:64]], axis=-1)

    wqa, wka, wva, wqb, wkb, wvb, wci, wcb, wcc = split_cols(w_in)
    win = jnp.concatenate([wqa, wka, swap_kv(wka), wva, swap_kv(wva), wqb, wkb, wvb, wci, wcb, wcc],
                          axis=-1).astype(BF16)
    wout = w_out.astype(BF16)
    wqt = jnp.swapaxes(peer_wq, 1, 2).astype(BF16)
    keys = peer_keys.reshape(depth, 2 * P_HEADS, N_KEYS, P_HALF).astype(BF16)
    u_tab = peer_u.astype(BF16)
    vt_tab = jnp.swapaxes(peer_v, 1, 2).astype(BF16)

    g64 = _block_diag_mean(384, HD)
    g32 = _block_diag_mean(384, B_DK)
    rope_a = _rope_tables(s_l, HD)
    rope_b = _rope_tables(s_l, B_DK)

    ck = cache_a_k.reshape(db, depth, past, 128)
    cv = cache_a_v.reshape(db, depth, past, 128)
    caches = (jnp.concatenate([ck, swap_kv(ck)], axis=-1).astype(BF16),
              jnp.concatenate([cv, swap_kv(cv)], axis=-1).astype(BF16),
              cache_b_k.reshape(db, depth, past, 384).astype(BF16),
              cache_b_v.reshape(db, depth, past, 384).astype(BF16))

    xp = x_prompt.reshape(nb * s_p, d)
    xs = x_sample.reshape(db * s_l, d)
    tps_p = s_p // tm_p
    tps_l = s_l // tm_l
    row_p = lambda i: 0
    row_l = lambda i: 1 + i // tps_l
    row_l_peer = lambda i: 1 + i // (s_l // tm_peer)

    new = [[], [], [], []]
    for l in range(depth):
        lam_init = 0.8 - 0.6 * math.exp(-0.3 * l)
        gains = ((jnp.tile(qn_a[l], A_HEADS) * HD ** -0.5)[None, :],
                 jnp.tile(kn_a[l], 2 * A_KV)[None, :],
                 (jnp.tile(qn_b[l], 2 * B_HEADS) * B_DK ** -0.5)[None, :],
                 jnp.tile(kn_b[l], 2 * B_HEADS)[None, :])
        lams = (lam_q1[l][None, :], lam_k1[l][None, :], lam_q2[l][None, :], lam_k2[l][None, :])
        gb = (jnp.tile(subln_g[l], B_HEADS) * (1.0 - lam_init))[None, :]
        post_args = (out_norm_a[l][None, :], gb, g64, out_norm_c[l][None, :], conv_w[l], conv_b[l][None, :],
                     wout)
        n1g = norm1_g[l][None, :]
        n2g = norm2_g[l][None, :]

        qa, ka2, va2, qb, kb, vb, u, cbg, cka, cva, ckb, cvb = _pre_mix(
            xp, mod[l], row_p, n1g, win, gains, (g64, g32), None, layer=l,
            tm=tm_pre_p, tiles_per_seq=1, emit_cache=True)
        new[0].append(cka.reshape(nb, s_p, A_KV, HD))
        new[1].append(cva.reshape(nb, s_p, A_KV, HD))
        new[2].append(ckb.reshape(nb, s_p, B_HEADS, 2, B_DK))
        new[3].append(cvb.reshape(nb, s_p, B_HEADS, 2 * B_DK))
        r3 = lambda a: a.reshape(nb, s_p, a.shape[-1])
        oa, ob = _attention(r3(qa), r3(ka2), r3(va2), r3(qb), r3(kb), r3(vb), None, lams,
                            layer=l, lam_init=lam_init, tq=tq_p)
        xp = _post_mix(oa.reshape(nb * s_p, 384), ob.reshape(nb * s_p, 384), u, cbg, xp, mod[l], row_p,
                       *post_args, layer=l, tm=tm_p, tiles_per_seq=tps_p)
        xp = _peer_split(xp, mod[l], row_p, n2g, wqt, keys, u_tab, vt_tab, layer=l, tm=tm_peer, ec=ec)

        qa, ka2, va2, qb, kb, vb, u, cbg = _pre_mix(
            xs, mod[l], row_l, n1g, win, gains, (g64, g32), rope_a + rope_b, layer=l,
            tm=tm_l, tiles_per_seq=tps_l, emit_cache=False)
        r3 = lambda a: a.reshape(db, s_l, a.shape[-1])
        oa, ob = _attention(r3(qa), r3(ka2), r3(va2), r3(qb), r3(kb), r3(vb), caches, lams,
                            layer=l, lam_init=lam_init, tq=tq_l)
        xs = _post_mix(oa.reshape(db * s_l, 384), ob.reshape(db * s_l, 384), u, cbg, xs, mod[l], row_l,
                       *post_args, layer=l, tm=tm_l, tiles_per_seq=tps_l)
        xs = _peer_split(xs, mod[l], row_l_peer, n2g, wqt, keys, u_tab, vt_tab, layer=l, tm=tm_peer, ec=ec)

    return (xp.reshape(nb, s_p, d), xs.reshape(db, s_l, d),
            jnp.stack(new[0], axis=1), jnp.stack(new[1], axis=1),
            jnp.stack(new[2], axis=1), jnp.stack(new[3], axis=1))
```

```python
import functools
import math

import numpy as np
import jax
import jax.numpy as jnp
from jax import lax
from jax.experimental import pallas as pl
from jax.experimental.pallas import tpu as pltpu

F32 = jnp.float32
BF16 = jnp.bfloat16

HD = 64
A_HEADS = 6
A_KV = 2
B_HEADS = 6
B_DK = 32
C_WIDTH = 256
A_WIDTH = A_HEADS * HD
B_WIDTH = B_HEADS * 2 * B_DK
GRID_W = 64
ROPE_BASE = 10000.0
EPS = 1e-6
P_HEADS = 8
N_KEYS = 128
P_HALF = 128
P_TOPK = 16

LANES = 128
SUBLANES = 8
VMEM_LIMIT = 48 * 1024 * 1024
TOKEN_TILE = 512
MIX_TILE = 1024
ATTN_Q_TILE = 128

_QA = (0, 384)
_KA2 = (384, 640)
_VA2 = (640, 896)
_QB = (896, 1280)
_KB = (1280, 1664)
_VB = (1664, 2048)
_CI = (2048, 2304)
_CB = (2304, 2560)
_CC = (2560, 2816)
NP_COLS = 2816

NEG_INF = float("-inf")


def _params(sem, vmem=VMEM_LIMIT):
    return pltpu.CompilerParams(dimension_semantics=sem, vmem_limit_bytes=vmem)


def _dot(a, b):
    return jnp.dot(a, b, preferred_element_type=F32)


def _dot_nt(a, b):
    return lax.dot_general(a, b, (((1,), (1,)), ((), ())), preferred_element_type=F32)


def _split3(a):
    hi = a.astype(BF16)
    r = a - hi.astype(F32)
    mid = r.astype(BF16)
    lo = (r - mid.astype(F32)).astype(BF16)
    return hi, mid, lo


def _dot_hp(a, b):
    a0, a1, a2 = _split3(a)
    b0, b1, b2 = _split3(b)
    return (_dot(a0, b0) + (_dot(a0, b1) + _dot(a1, b0))
            + (_dot(a0, b2) + _dot(a1, b1) + _dot(a2, b0)))


def _group_mean_sq(v, g_ref):
    v2 = v * v
    hi = v2.astype(BF16)
    lo = (v2 - hi.astype(F32)).astype(BF16)
    g = g_ref[...]
    return _dot(hi, g) + _dot(lo, g)


def _rms(x):
    return x * lax.rsqrt(jnp.mean(x * x, axis=-1, keepdims=True) + EPS)


def _mod_kernel(cond_ref, w_ref, b_ref, o_ref):
    c = cond_ref[...]
    s = c / (1.0 + jnp.exp(-c))
    o_ref[...] = _dot_hp(s, w_ref[...]) + b_ref[...]


def _modulation(cond, ada_w, ada_b):
    depth, d, d6 = ada_w.shape
    n = cond.shape[0]
    tn = 1536
    return pl.pallas_call(
        _mod_kernel,
        grid=(depth, d6 // tn),
        in_specs=[pl.BlockSpec((n, d), lambda l, j: (0, 0)),
                  pl.BlockSpec((None, d, tn), lambda l, j: (l, 0, j)),
                  pl.BlockSpec((None, 1, tn), lambda l, j: (l, 0, j))],
        out_specs=pl.BlockSpec((None, n, tn), lambda l, j: (l, 0, j)),
        out_shape=jax.ShapeDtypeStruct((depth, n, d6), F32),
        compiler_params=_params(("parallel", "parallel")),
        name="modulation",
    )(cond, ada_w, ada_b.reshape(depth, 1, d6))


def _rope_block(blk, c, sm, sp, shift):
    return (blk * c + pltpu.roll(blk, LANES - shift, 1) * sm + pltpu.roll(blk, shift, 1) * sp)


def _pre_mix_kernel(*refs, rope, emit_cache):
    it = iter(refs)
    x_ref, mod_ref, n1g_ref, win_ref = next(it), next(it), next(it), next(it)
    gqa_ref, gka_ref, gqb_ref, gkb_ref = next(it), next(it), next(it), next(it)
    g64_ref, g32_ref = next(it), next(it)
    if rope:
        ca_ref, sma_ref, spa_ref = next(it), next(it), next(it)
        cb_ref, smb_ref, spb_ref = next(it), next(it), next(it)
    qa_o, ka_o, va_o, qb_o, kb_o, vb_o, u_o, cbg_o = (next(it) for _ in range(8))
    if emit_cache:
        cka_o, cva_o, ckb_o, cvb_o = (next(it) for _ in range(4))

    x = x_ref[...]
    sh1 = mod_ref[0:1, :]
    sc1 = mod_ref[1:2, :]
    h = _rms(x) * n1g_ref[...] * (1.0 + sc1) + sh1
    hb = h.astype(BF16)

    def proj(seg):
        return _dot(hb, win_ref[:, seg[0]:seg[1]])

    def qk(seg, g_ref, gain_ref, width):
        p = proj(seg)
        ms = _group_mean_sq(p, g_ref) if width == 384 else _dot_group256(p, g_ref)
        return p * lax.rsqrt(ms + EPS), gain_ref[...]

    def _dot_group256(p, g_ref):
        v2 = p * p
        hi = v2.astype(BF16)
        lo = (v2 - hi.astype(F32)).astype(BF16)
        g = g_ref[0:256, 0:256]
        return _dot(hi, g) + _dot(lo, g)

    def finish(y, gain, out_ref, tabs, shift, cache_ref=None, cache_cols=None):
        nblk = y.shape[1] // LANES
        for m in range(nblk):
            sl = slice(m * LANES, (m + 1) * LANES)
            blk = y[:, sl] * gain[:, sl]
            if cache_ref is not None and m < cache_cols // LANES:
                cache_ref[:, sl] = blk
            if tabs is not None:
                blk = _rope_block(blk, tabs[0][...], tabs[1][...], tabs[2][...], shift)
            out_ref[:, sl] = blk.astype(out_ref.dtype)

    tabs_a = (ca_ref, sma_ref, spa_ref) if rope else None
    tabs_b = (cb_ref, smb_ref, spb_ref) if rope else None

    y, g = qk(_QA, g64_ref, gqa_ref, 384)
    finish(y, g, qa_o, tabs_a, HD // 4)
    y, g = qk(_KA2, g64_ref, gka_ref, 256)
    finish(y, g, ka_o, tabs_a, HD // 4, cka_o if emit_cache else None, 128)
    va = proj(_VA2)
    va_o[...] = va.astype(va_o.dtype)
    if emit_cache:
        cva_o[...] = va[:, 0:128]
    y, g = qk(_QB, g32_ref, gqb_ref, 384)
    finish(y, g, qb_o, tabs_b, B_DK // 4)
    y, g = qk(_KB, g32_ref, gkb_ref, 384)
    finish(y, g, kb_o, tabs_b, B_DK // 4, ckb_o if emit_cache else None, 384)
    vb = proj(_VB)
    vb_o[...] = vb.astype(vb_o.dtype)
    if emit_cache:
        cvb_o[...] = vb
    u_o[...] = proj(_CC) * proj(_CI)
    cbg_o[...] = proj(_CB)


def _pre_mix(x, mod_l, row_of_tile, n1g, win, gains, gmats, rope_tabs, *, layer, tm, tiles_per_seq, emit_cache):
    t, d = x.shape
    nt = t // tm
    rope = rope_tabs is not None
    const = lambda shape: pl.BlockSpec(shape, lambda i: (0,) * len(shape))
    in_specs = [pl.BlockSpec((tm, d), lambda i: (i, 0)),
                pl.BlockSpec((None, 6, d), lambda i: (row_of_tile(i), 0, 0)),
                const((1, d)), pl.BlockSpec((None, d, NP_COLS), lambda i: (layer, 0, 0)),
                const((1, 384)), const((1, 256)), const((1, 384)), const((1, 384)),
                const((384, 384)), const((384, 384))]
    args = [x, mod_l, n1g, win, *gains, *gmats]
    if rope:
        in_specs += [pl.BlockSpec((tm, LANES), lambda i: (i % tiles_per_seq, 0))] * 6
        args += list(rope_tabs)
    widths = [(384, BF16), (256, BF16), (256, BF16), (384, BF16), (384, BF16), (384, BF16),
              (256, F32), (256, F32)]
    if emit_cache:
        widths += [(128, F32), (128, F32), (384, F32), (384, F32)]
    out_shape = [jax.ShapeDtypeStruct((t, w), dt) for w, dt in widths]
    out_specs = [pl.BlockSpec((tm, w), lambda i: (i, 0)) for w, _ in widths]
    return pl.pallas_call(
        functools.partial(_pre_mix_kernel, rope=rope, emit_cache=emit_cache),
        grid=(nt,), in_specs=in_specs, out_specs=out_specs, out_shape=out_shape,
        compiler_params=_params(("parallel",)),
        name="pre_mix_rope" if rope else "pre_mix",
    )(*args)


def _softmax_pv(q, k_parts, v_parts):
    ss = [_dot_nt(q, k) for k in k_parts]
    m = jnp.max(ss[0], axis=-1, keepdims=True)
    for s in ss[1:]:
        m = jnp.maximum(m, jnp.max(s, axis=-1, keepdims=True))
    l = None
    o = None
    for s, v in zip(ss, v_parts):
        p = jnp.exp(s - m)
        ls = jnp.sum(p, axis=-1, keepdims=True)
        os_ = _dot(p.astype(BF16), v)
        l = ls if l is None else l + ls
        o = os_ if o is None else o + os_
    return o * (1.0 / l)


def _attn_kernel(*refs, has_cache, lam_init):
    it = iter(refs)
    qa_ref, ka_ref, va_ref, qb_ref, kb_ref, vb_ref = (next(it) for _ in range(6))
    if has_cache:
        cka_ref, cva_ref, ckb_ref, cvb_ref = (next(it) for _ in range(4))
    lq1_ref, lk1_ref, lq2_ref, lk2_ref = (next(it) for _ in range(4))
    oa_ref, ob_ref = next(it), next(it)

    tq = qa_ref.shape[0]
    lane = lax.broadcasted_iota(jnp.int32, (tq, LANES), 1)
    lo_half = lane < 64

    lam = (jnp.exp(jnp.sum(lq1_ref[...] * lk1_ref[...], axis=-1, keepdims=True))
           - jnp.exp(jnp.sum(lq2_ref[...] * lk2_ref[...], axis=-1, keepdims=True)) + lam_init)

    o_a = [None] * A_HEADS
    for kblk in range(2):
        heads = [h for h in range(A_HEADS) if (0 if h // (A_HEADS // A_KV) == h % 2 else 1) == kblk]
        ksl = slice(kblk * LANES, (kblk + 1) * LANES)
        qs = []
        for h in heads:
            qblk = qa_ref[:, (h // 2) * LANES:(h // 2 + 1) * LANES]
            keep = lo_half if h % 2 == 0 else jnp.logical_not(lo_half)
            qs.append(jnp.where(keep, qblk, jnp.zeros_like(qblk)))
        ks, vs = [ka_ref[:, ksl]], [va_ref[:, ksl]]
        if has_cache:
            ks.append(cka_ref[:, ksl])
            vs.append(cva_ref[:, ksl])
        o = _softmax_pv(jnp.concatenate(qs, axis=0), ks, vs)
        for k, h in enumerate(heads):
            o_a[h] = o[k * tq:(k + 1) * tq, :]
    for m in range(A_HEADS // 2):
        oa_ref[:, m * LANES:(m + 1) * LANES] = jnp.where(lo_half, o_a[2 * m], o_a[2 * m + 1])

    seg_id = lane >> 5
    for m in range(B_HEADS // 2):
        sl = slice(m * LANES, (m + 1) * LANES)
        qblk = qb_ref[:, sl]
        ks, vs = [kb_ref[:, sl]], [vb_ref[:, sl]]
        if has_cache:
            ks.append(ckb_ref[:, sl])
            vs.append(cvb_ref[:, sl])
        qs = [jnp.where(seg_id == seg, qblk, jnp.zeros_like(qblk)) for seg in range(4)]
        o = _softmax_pv(jnp.concatenate(qs, axis=0), ks, vs)
        outs = [o[(2 * par) * tq:(2 * par + 1) * tq, :] - lam * o[(2 * par + 1) * tq:(2 * par + 2) * tq, :]
                for par in range(2)]
        ob_ref[:, sl] = jnp.where(lo_half, outs[0], outs[1])


def _attention(qa, ka2, va2, qb, kb, vb, caches, lams, *, layer, lam_init, tq):
    b, s, _ = qa.shape
    has_cache = caches is not None
    qspec = lambda w: pl.BlockSpec((None, tq, w), lambda i, j: (i, j, 0))
    kspec = lambda w: pl.BlockSpec((None, s, w), lambda i, j: (i, 0, 0))
    in_specs = [qspec(384), kspec(256), kspec(256), qspec(384), kspec(384), kspec(384)]
    args = [qa, ka2, va2, qb, kb, vb]
    if has_cache:
        lc = caches[0].shape[2]
        cspec = lambda w: pl.BlockSpec((None, None, lc, w), lambda i, j: (i, layer, 0, 0))
        in_specs += [cspec(256), cspec(256), cspec(384), cspec(384)]
        args += list(caches)
    in_specs += [pl.BlockSpec((1, B_DK), lambda i, j: (0, 0))] * 4
    args += list(lams)
    return pl.pallas_call(
        functools.partial(_attn_kernel, has_cache=has_cache, lam_init=lam_init),
        grid=(b, s // tq), in_specs=in_specs,
        out_specs=[qspec(384), qspec(384)],
        out_shape=[jax.ShapeDtypeStruct((b, s, 384), F32)] * 2,
        compiler_params=_params(("parallel", "parallel")),
        name="attention_cached" if has_cache else "attention",
    )(*args)


def _post_mix_kernel(oa_ref, ob_ref, u_ref, up_ref, un_ref, cbg_ref, x_ref, mod_ref,
                     ga_ref, gb_ref, g64_ref, gc_ref, cw_ref, cbias_ref, wout_ref, o_ref,
                     *, tiles_per_seq):
    i = pl.program_id(0)
    tm = x_ref.shape[0]
    pos = i % tiles_per_seq
    u = u_ref[...]
    row = lax.broadcasted_iota(jnp.int32, u.shape, 0)
    prev_row = jnp.where(pos == 0, 0.0, 1.0) * up_ref[SUBLANES - 1:SUBLANES, :]
    next_row = jnp.where(pos == tiles_per_seq - 1, 0.0, 1.0) * un_ref[0:1, :]
    u_dn = jnp.where(row == 0, prev_row, pltpu.roll(u, 1, 0))
    u_up = jnp.where(row == tm - 1, next_row, pltpu.roll(u, tm - 1, 0))
    y = u_dn * cw_ref[0:1, :] + u * cw_ref[1:2, :] + u_up * cw_ref[2:3, :] + cbias_ref[...]
    oc = cbg_ref[...] * y

    na = _rms(oa_ref[...]) * ga_ref[...]
    ob = ob_ref[...]
    nb = ob * lax.rsqrt(_group_mean_sq(ob, g64_ref) + EPS) * gb_ref[...]
    nc = _rms(oc) * gc_ref[...]
    mix = (_dot(na.astype(BF16), wout_ref[0:384, :]) + _dot(nb.astype(BF16), wout_ref[384:768, :])
           + _dot(nc.astype(BF16), wout_ref[768:1024, :]))
    o_ref[...] = x_ref[...] + mod_ref[2:3, :] * mix


def _post_mix(oa, ob, u, cbg, x, mod_l, row_of_tile, ga, gb, g64, gc, cw, cbias, wout, *, layer, tm, tiles_per_seq):
    t, d = x.shape
    nt = t // tm
    nb8 = t // SUBLANES
    step = tm // SUBLANES
    const = lambda shape: pl.BlockSpec(shape, lambda i: (0,) * len(shape))
    tok = lambda w: pl.BlockSpec((tm, w), lambda i: (i, 0))
    in_specs = [tok(384), tok(384), tok(256),
                pl.BlockSpec((SUBLANES, 256), lambda i: (jnp.maximum(i * step - 1, 0), 0)),
                pl.BlockSpec((SUBLANES, 256), lambda i: (jnp.minimum((i + 1) * step, nb8 - 1), 0)),
                tok(256), tok(d),
                pl.BlockSpec((None, 6, d), lambda i: (row_of_tile(i), 0, 0)),
                const((1, 384)), const((1, 384)), const((384, 384)), const((1, 256)),
                const((3, 256)), const((1, 256)), pl.BlockSpec((None, d, d), lambda i: (layer, 0, 0))]
    return pl.pallas_call(
        functools.partial(_post_mix_kernel, tiles_per_seq=tiles_per_seq),
        grid=(nt,), in_specs=in_specs, out_specs=tok(d),
        out_shape=jax.ShapeDtypeStruct((t, d), F32),
        compiler_params=_params(("parallel",)),
        name="post_mix",
    )(oa, ob, u, u, u, cbg, x, mod_l, ga, gb, g64, gc, cw, cbias, wout)


def _gelu(x):
    k1 = -2.0 * math.sqrt(2.0 / math.pi) * math.log2(math.e)
    return x / (1.0 + jnp.exp2(x * (k1 + (k1 * 0.044715) * (x * x))))


BF16_SUBLANES = 16


def _bcast_row_bf16(row):
    packed = jnp.broadcast_to(row, (BF16_SUBLANES, LANES)).astype(BF16)
    return jnp.concatenate([packed] * (N_KEYS // BF16_SUBLANES), axis=0)


def _top16(s):
    n = s.shape[0]
    rows = lax.broadcasted_iota(jnp.int32, s.shape, 0).astype(F32)
    work = s
    rank = jnp.full(s.shape, float(P_TOPK), F32)
    vals = []
    for a in range(P_TOPK):
        m = jnp.max(work, axis=0, keepdims=True)
        idx = jnp.min(jnp.where(work == m, rows, float(n)), axis=0, keepdims=True)
        hit = rows == idx
        rank = jnp.where(hit, float(a), rank)
        work = jnp.where(hit, NEG_INF, work)
        vals.append(m)
    return vals, rank


def _top16_distinct_pair(s1, s2):
    work1, work2 = s1, s2
    rank2 = jnp.full(s2.shape, float(P_TOPK), F32)
    vals1, vals2 = [], []
    for a in range(P_TOPK):
        m1 = jnp.max(work1, axis=0, keepdims=True)
        m2 = jnp.max(work2, axis=0, keepdims=True)
        hit2 = work2 == m2
        work1 = jnp.where(work1 == m1, NEG_INF, work1)
        rank2 = jnp.where(hit2, float(a), rank2)
        work2 = jnp.where(hit2, NEG_INF, work2)
        vals1.append(m1)
        vals2.append(m2)
    return vals1, vals2, rank2


def _stack16(rows_list):
    w = rows_list[0].shape[1]
    r = lax.broadcasted_iota(jnp.int32, (P_TOPK, w), 0)
    out = jnp.zeros((P_TOPK, w), F32)
    for a, v in enumerate(rows_list):
        out = jnp.where(r == a, v, out)
    return out


_CAND_ROWS = 80


def _cand_build(v1_rows, v1_arr, v2_arr, v2_row0, op):
    pieces = [op(v1_rows[0], v2_arr), op(v1_rows[1], v2_arr[0:8])]
    for a in range(2, 8):
        pieces.append(op(v1_rows[a], v2_arr[0:8]))
    pieces.append(op(v1_arr[8:16], v2_row0))
    return jnp.concatenate(pieces, axis=0)


def _cand_index(w):
    row = lax.broadcasted_iota(jnp.int32, (_CAND_ROWS, w), 0)
    a = jnp.where(row < 16, 0, jnp.where(row < 72, 1 + ((row - 16) >> 3), 8 + (row - 72)))
    b = jnp.where(row < 16, row, jnp.where(row < 72, (row - 16) & 7, 0))
    valid = (a + 1) * (b + 1) <= P_TOPK
    flat = (a * P_TOPK + b).astype(F32)
    return valid, flat


def _peer_head_tables(s1, s2, valid, flat, exact):
    if exact:
        v1_rows, rank1 = _top16(s1)
        v2_rows, rank2 = _top16(s2)
    else:
        v1_rows, v2_rows, rank2 = _top16_distinct_pair(s1, s2)
    v1 = _stack16(v1_rows)
    v2 = _stack16(v2_rows)
    e1 = jnp.exp(v1 - v1_rows[0])
    e2 = jnp.exp(v2 - v2_rows[0])
    e1_rows = [e1[a:a + 1, :] for a in range(8)]
    cand = _cand_build(v1_rows, v1, v2, v2_rows[0], lambda p, q: p + q)
    gcand = _cand_build(e1_rows, e1, e2, e2[0:1, :], lambda p, q: p * q)
    work = jnp.where(valid, cand, NEG_INF)
    sel = jnp.zeros_like(cand)
    for _ in range(P_TOPK):
        m = jnp.max(work, axis=0, keepdims=True)
        idx = jnp.min(jnp.where(work == m, flat, 1e9), axis=0, keepdims=True)
        hit = flat == idx
        sel = jnp.where(hit, 1.0, sel)
        work = jnp.where(hit, NEG_INF, work)
    z = jnp.sum(sel * gcand, axis=0, keepdims=True)
    counts = [jnp.sum(sel[0:16], axis=0, keepdims=True)]
    for a in range(1, 8):
        counts.append(jnp.sum(sel[8 + 8 * a:16 + 8 * a], axis=0, keepdims=True))
    for a in range(8, 16):
        counts.append(sel[64 + a:65 + a, :])
    ltab = jnp.zeros_like(s1)
    if exact:
        member = rank1 < float(P_TOPK)
        for a in range(P_TOPK):
            ltab = jnp.where(rank1 == float(a), counts[a], ltab)
        bad = None
    else:
        member = s1 >= v1_rows[P_TOPK - 1]
        for a in range(P_TOPK):
            ltab = jnp.where(s1 == v1_rows[a], counts[a], ltab)
        n1 = jnp.sum(jnp.where(member, 1.0, 0.0), axis=0, keepdims=True)
        n2 = jnp.sum(jnp.where(rank2 < float(P_TOPK), 1.0, 0.0), axis=0, keepdims=True)
        bad = jnp.where(n1 + n2 != 2.0 * P_TOPK, 1.0, 0.0)
    atab = jnp.where(member, jnp.exp(s1 - v1_rows[0]) * (1.0 / z), 0.0)
    btab = jnp.exp(s2 - v2_rows[0])
    return ltab, atab, rank2, btab, bad


def _peer_prep(x_ref, mod_ref, n2g_ref, wqt_ref, ht_scr, qt_scr):
    tm = x_ref.shape[0]
    ncb = tm // LANES
    nq = wqt_ref.shape[0]
    x = x_ref[...]
    h = _rms(x) * n2g_ref[...] * (1.0 + mod_ref[4:5, :]) + mod_ref[3:4, :]
    htb = h.T.astype(BF16)
    ht_scr[...] = htb
    for half in range(2):
        rs = slice(half * (nq // 2), (half + 1) * (nq // 2))
        qt = _dot(wqt_ref[rs, :], htb)
        for cb in range(ncb):
            qt_scr[cb, rs, :] = qt[:, cb * LANES:(cb + 1) * LANES]


def _peer_head_block(hd, cb, keys_ref, qt_scr, tabs, exact):
    rank2_scr, b_scr, l_scr, a_scr = tabs
    ncb = qt_scr.shape[0]
    valid, flat = _cand_index(LANES)
    r1 = pl.multiple_of(hd * (2 * P_HALF), 2 * P_HALF)
    q1 = qt_scr[cb, pl.ds(r1, P_HALF), :].astype(BF16)
    q2 = qt_scr[cb, pl.ds(r1 + P_HALF, P_HALF), :].astype(BF16)
    s1 = _dot(keys_ref[2 * hd], q1)
    s2 = _dot(keys_ref[2 * hd + 1], q2)
    ltab, atab, rank2, btab, bad = _peer_head_tables(s1, s2, valid, flat, exact)
    n = hd * ncb + cb
    l_scr[n] = ltab
    a_scr[n] = atab
    rank2_scr[n] = rank2.astype(rank2_scr.dtype)
    b_scr[n] = btab.astype(b_scr.dtype)
    return bad


def _peer_head(hd, keys_ref, qt_scr, tabs, bad_scr, exact):
    bad = jnp.zeros((1, LANES), F32)
    for cb in range(qt_scr.shape[0]):
        bad_cb = _peer_head_block(hd, cb, keys_ref, qt_scr, tabs, exact)
        if not exact:
            bad = jnp.maximum(bad, bad_cb)
    if not exact:
        bad_scr[hd] = jnp.broadcast_to(bad, (SUBLANES, LANES))


def _peer_head_redo(hd, keys_ref, qt_scr, tabs, bad_scr):
    @pl.when(jnp.max(bad_scr[hd]) > 0.0)
    def _():
        _peer_head(hd, keys_ref, qt_scr, tabs, bad_scr, exact=True)


PEER_SUB = 512
MXU_COLS = 256
PEER_VMEM_LIMIT = 56 * 1024 * 1024


def _peer_kernel(x_ref, mod_ref, n2g_ref, wqt_ref, keys_ref, u_ref, vt_ref, o_ref,
                 ht_scr, qt_scr, rank2_scr, b_scr, l_scr, a_scr, acc_scr, bad_scr):
    j = pl.program_id(1)
    nj = pl.num_programs(1)
    tm = x_ref.shape[0]
    ncb = tm // LANES
    ec = u_ref.shape[0]
    rows_per_sub = PEER_SUB // N_KEYS
    tabs = (rank2_scr, b_scr, l_scr, a_scr)

    @pl.when(j == 0)
    def _():
        acc_scr[...] = jnp.zeros_like(acc_scr)
        _peer_prep(x_ref, mod_ref, n2g_ref, wqt_ref, ht_scr, qt_scr)

        def head_body(hd, carry):
            _peer_head(hd, keys_ref, qt_scr, tabs, bad_scr, exact=False)
            return carry

        def redo_body(hd, carry):
            _peer_head_redo(hd, keys_ref, qt_scr, tabs, bad_scr)
            return carry

        lax.fori_loop(0, P_HEADS, head_body, 0)
        lax.fori_loop(0, P_HEADS, redo_body, 0)

    n_half = 2
    eh = ec // n_half
    for nb in range(tm // MXU_COLS):
        for half in range(n_half):
            ns = slice(nb * MXU_COLS, (nb + 1) * MXU_COLS)
            act = _dot(u_ref[half * eh:(half + 1) * eh, :], ht_scr[:, ns])
            g_rows = []
            for q in range(eh // PEER_SUB):
                cols = []
                for c2 in range(MXU_COLS // LANES):
                    cb = nb * (MXU_COLS // LANES) + c2
                    ws = [jnp.zeros((N_KEYS, LANES), BF16) for _ in range(rows_per_sub)]
                    for hd in range(P_HEADS):
                        n = hd * ncb + cb
                        rank2 = rank2_scr[n]
                        btab = b_scr[n]
                        for r in range(rows_per_sub):
                            i1 = j * (ec // N_KEYS) + (half * eh + q * PEER_SUB) // N_KEYS + r
                            lrow = _bcast_row_bf16(l_scr[n, pl.ds(i1, 1), :])
                            arow = _bcast_row_bf16(a_scr[n, pl.ds(i1, 1), :])
                            ws[r] = ws[r] + jnp.where(rank2 < lrow, btab * arow, jnp.zeros_like(btab))
                    gs = []
                    for r in range(rows_per_sub):
                        r0 = q * PEER_SUB + r * N_KEYS
                        a_blk = act[r0:r0 + N_KEYS, c2 * LANES:(c2 + 1) * LANES]
                        gs.append(ws[r] * _gelu(a_blk.astype(BF16)))
                    cols.append(jnp.concatenate(gs, axis=0))
                g_rows.append(jnp.concatenate(cols, axis=1))
            g = jnp.concatenate(g_rows, axis=0)
            acc_scr[:, ns] += _dot(vt_ref[:, half * eh:(half + 1) * eh], g)

    @pl.when(j == nj - 1)
    def _():
        o_ref[...] = x_ref[...] + mod_ref[5:6, :] * acc_scr[...].T


def _peer(x, mod_l, row_of_tile, n2g, wqt, keys, u_tab, vt_tab, *, layer, tm, ec):
    t, d = x.shape
    nt = t // tm
    ne = u_tab.shape[1]
    ncb = tm // LANES
    nq = wqt.shape[1]
    tabs = lambda dt: pltpu.VMEM((P_HEADS * ncb, N_KEYS, LANES), dt)
    return pl.pallas_call(
        _peer_kernel,
        grid=(nt, ne // ec),
        in_specs=[pl.BlockSpec((tm, d), lambda i, j: (i, 0)),
                  pl.BlockSpec((None, 6, d), lambda i, j: (row_of_tile(i), 0, 0)),
                  pl.BlockSpec((1, d), lambda i, j: (0, 0)),
                  pl.BlockSpec((None, nq, d), lambda i, j: (layer, 0, 0), pipeline_mode=pl.Buffered(1)),
                  pl.BlockSpec((None, 2 * P_HEADS, N_KEYS, P_HALF), lambda i, j: (layer, 0, 0, 0),
                               pipeline_mode=pl.Buffered(1)),
                  pl.BlockSpec((None, ec, d), lambda i, j: (layer, j, 0)),
                  pl.BlockSpec((None, d, ec), lambda i, j: (layer, 0, j))],
        out_specs=pl.BlockSpec((tm, d), lambda i, j: (i, 0)),
        out_shape=jax.ShapeDtypeStruct((t, d), F32),
        scratch_shapes=[pltpu.VMEM((d, tm), BF16),
                        pltpu.VMEM((ncb, nq, LANES), F32),
                        tabs(BF16), tabs(BF16), tabs(F32), tabs(F32),
                        pltpu.VMEM((d, tm), F32),
                        pltpu.VMEM((P_HEADS, SUBLANES, LANES), F32)],
        compiler_params=_params(("parallel", "arbitrary"), PEER_VMEM_LIMIT),
        name="peer",
    )(x, mod_l, n2g, wqt, keys, u_tab, vt_tab)


def _route_kernel(x_ref, mod_ref, n2g_ref, wqt_ref, keys_ref,
                  ht_ref, rank2_ref, b_ref, l_ref, a_ref, qt_scr, bad_scr):
    tabs = (rank2_ref, b_ref, l_ref, a_ref)
    hd = pl.program_id(1)

    @pl.when(hd == 0)
    def _():
        _peer_prep(x_ref, mod_ref, n2g_ref, wqt_ref, ht_ref, qt_scr)

    _peer_head(hd, keys_ref, qt_scr, tabs, bad_scr, exact=False)
    _peer_head_redo(hd, keys_ref, qt_scr, tabs, bad_scr)


def _experts_kernel(x_ref, mod_ref, ht_scr, rank2_scr, b_scr, l_scr, a_scr, u_ref, vt_ref, o_ref, acc_scr):
    j = pl.program_id(1)
    nj = pl.num_programs(1)
    tm = x_ref.shape[0]
    ncb = tm // LANES
    ec = u_ref.shape[0]
    rows_per_sub = PEER_SUB // N_KEYS

    @pl.when(j == 0)
    def _():
        acc_scr[...] = jnp.zeros_like(acc_scr)

    n_half = 2
    eh = ec // n_half
    for nb in range(tm // MXU_COLS):
        for half in range(n_half):
            ns = slice(nb * MXU_COLS, (nb + 1) * MXU_COLS)
            act = _dot(u_ref[half * eh:(half + 1) * eh, :], ht_scr[:, ns])
            g_rows = []
            for q in range(eh // PEER_SUB):
                cols = []
                for c2 in range(MXU_COLS // LANES):
                    cb = nb * (MXU_COLS // LANES) + c2
                    ws = [jnp.zeros((N_KEYS, LANES), BF16) for _ in range(rows_per_sub)]
                    for hd in range(P_HEADS):
                        n = hd * ncb + cb
                        rank2 = rank2_scr[n]
                        btab = b_scr[n]
                        for r in range(rows_per_sub):
                            i1 = j * (ec // N_KEYS) + (half * eh + q * PEER_SUB) // N_KEYS + r
                            lrow = _bcast_row_bf16(l_scr[n, pl.ds(i1, 1), :])
                            arow = _bcast_row_bf16(a_scr[n, pl.ds(i1, 1), :])
                            ws[r] = ws[r] + jnp.where(rank2 < lrow, btab * arow, jnp.zeros_like(btab))
                    gs = []
                    for r in range(rows_per_sub):
                        r0 = q * PEER_SUB + r * N_KEYS
                        a_blk = act[r0:r0 + N_KEYS, c2 * LANES:(c2 + 1) * LANES]
                        gs.append(ws[r] * _gelu(a_blk.astype(BF16)))
                    cols.append(jnp.concatenate(gs, axis=0))
                g_rows.append(jnp.concatenate(cols, axis=1))
            g = jnp.concatenate(g_rows, axis=0)
            acc_scr[:, ns] += _dot(vt_ref[:, half * eh:(half + 1) * eh], g)

    @pl.when(j == nj - 1)
    def _():
        o_ref[...] = x_ref[...] + mod_ref[5:6, :] * acc_scr[...].T


def _peer_split(x, mod_l, row_of_tile, n2g, wqt, keys, u_tab, vt_tab, *, layer, tm, ec):
    t, d = x.shape
    nt = t // tm
    ne = u_tab.shape[1]
    ncb = tm // LANES
    nq = wqt.shape[1]
    n_tab = P_HEADS * ncb
    tab_shape = lambda dt: jax.ShapeDtypeStruct((nt * n_tab, N_KEYS, LANES), dt)
    tab_spec1 = pl.BlockSpec((n_tab, N_KEYS, LANES), lambda i, h: (i, 0, 0))
    ht, rank2, btab, ltab, atab = pl.pallas_call(
        _route_kernel,
        grid=(nt, P_HEADS),
        in_specs=[pl.BlockSpec((tm, d), lambda i, h: (i, 0)),
                  pl.BlockSpec((None, 6, d), lambda i, h: (row_of_tile(i), 0, 0)),
                  pl.BlockSpec((1, d), lambda i, h: (0, 0)),
                  pl.BlockSpec((None, nq, d), lambda i, h: (layer, 0, 0)),
                  pl.BlockSpec((None, 2 * P_HEADS, N_KEYS, P_HALF), lambda i, h: (layer, 0, 0, 0))],
        out_specs=[pl.BlockSpec((d, tm), lambda i, h: (0, i)), tab_spec1, tab_spec1, tab_spec1, tab_spec1],
        out_shape=[jax.ShapeDtypeStruct((d, t), BF16), tab_shape(BF16), tab_shape(BF16),
                   tab_shape(F32), tab_shape(F32)],
        scratch_shapes=[pltpu.VMEM((ncb, nq, LANES), F32),
                        pltpu.VMEM((P_HEADS, SUBLANES, LANES), F32)],
        compiler_params=_params(("parallel", "arbitrary")),
        name="peer_route",
    )(x, mod_l, n2g, wqt, keys)
    tab_spec2 = pl.BlockSpec((n_tab, N_KEYS, LANES), lambda i, j: (i, 0, 0))
    return pl.pallas_call(
        _experts_kernel,
        grid=(nt, ne // ec),
        in_specs=[pl.BlockSpec((tm, d), lambda i, j: (i, 0)),
                  pl.BlockSpec((None, 6, d), lambda i, j: (row_of_tile(i), 0, 0)),
                  pl.BlockSpec((d, tm), lambda i, j: (0, i)),
                  tab_spec2, tab_spec2, tab_spec2, tab_spec2,
                  pl.BlockSpec((None, ec, d), lambda i, j: (layer, j, 0)),
                  pl.BlockSpec((None, d, ec), lambda i, j: (layer, 0, j))],
        out_specs=pl.BlockSpec((tm, d), lambda i, j: (i, 0)),
        out_shape=jax.ShapeDtypeStruct((t, d), F32),
        scratch_shapes=[pltpu.VMEM((d, tm), F32)],
        compiler_params=_params(("parallel", "arbitrary"), PEER_VMEM_LIMIT),
        name="peer_experts",
    )(x, mod_l, ht, rank2, btab, ltab, atab, u_tab, vt_tab)


def _rope_tables(n_tokens, dim):
    quarter = dim // 4
    pos = np.arange(n_tokens)
    row = (pos // GRID_W).astype(np.float32)
    col = (pos % GRID_W).astype(np.float32)
    lane = np.arange(LANES)
    dd = lane % dim
    half = dd // (dim // 2)
    e = dd % (dim // 2)
    second = e // quarter
    freq = e % quarter
    inv = jnp.asarray(ROPE_BASE, F32) ** (-jnp.arange(quarter, dtype=F32) / quarter)
    inv_lane = inv[freq]
    p = jnp.where(jnp.asarray(half == 0)[None, :], jnp.asarray(row)[:, None], jnp.asarray(col)[:, None])
    ang = p * inv_lane[None, :]
    cos, sin = jnp.cos(ang), jnp.sin(ang)
    first = jnp.asarray(second == 0)[None, :]
    return cos, jnp.where(first, -sin, 0.0), jnp.where(first, 0.0, sin)


def _block_diag_mean(width, group):
    idx = np.arange(width) // group
    return jnp.asarray((idx[:, None] == idx[None, :]).astype(np.float32) / group, BF16)


def kernel(x_prompt, x_sample, cache_a_k, cache_a_v, cache_b_k, cache_b_v, c, c_ctx, ada_w, ada_b, norm1_g, norm2_g, w_in, qn_a, kn_a, qn_b, kn_b, lam_q1, lam_k1, lam_q2, lam_k2, subln_g, out_norm_a, out_norm_c, conv_w, conv_b, w_out, peer_wq, peer_keys, peer_u, peer_v):
    nb, s_p, d = x_prompt.shape
    db, s_l, _ = x_sample.shape
    depth = w_in.shape[0]
    past = cache_a_k.shape[2]
    tm_p = s_p
    tm_l = min(MIX_TILE, s_l)
    tq_l = min(ATTN_Q_TILE, s_l)
    tq_p = min(ATTN_Q_TILE, s_p)
    assert s_l % tq_l == 0 and s_p % tq_p == 0
    tm_pre_p = min(MIX_TILE, nb * s_p)
    tm_peer = TOKEN_TILE
    ec = 2048
    assert s_l % tm_l == 0 and s_l % GRID_W == 0 and d == 1024 and (nb * s_p) % tm_pre_p == 0
    assert s_l % tm_peer == 0 and (nb * s_p) % tm_peer == 0 and tm_p % SUBLANES == 0

    n_cond = 1 + db
    n_rows = -(-n_cond // SUBLANES) * SUBLANES
    cond = jnp.concatenate([c_ctx[None, :], c, jnp.zeros((n_rows - n_cond, d), F32)], axis=0)
    mod = _modulation(cond, ada_w, ada_b).reshape(depth, n_rows, 6, d)

    def split_cols(w):
        offs = np.cumsum([0, 384, 128, 128, 384, 384, 384, 256, 256, 256])
        return [w[..., offs[k]:offs[k + 1]] for k in range(9)]

    def swap_kv(w):
        return jnp.concatenate([w[..., 64:128], w[..., 0:64]], axis=-1)

    wqa, wka, wva, wqb, wkb, wvb, wci, wcb, wcc = split_cols(w_in)
    win = jnp.concatenate([wqa, wka, swap_kv(wka), wva, swap_kv(wva), wqb, wkb, wvb, wci, wcb, wcc],
                          axis=-1).astype(BF16)
    wout = w_out.astype(BF16)
    wqt = jnp.swapaxes(peer_wq, 1, 2).astype(BF16)
    keys = peer_keys.reshape(depth, 2 * P_HEADS, N_KEYS, P_HALF).astype(BF16)
    u_tab = peer_u.astype(BF16)
    vt_tab = jnp.swapaxes(peer_v, 1, 2).astype(BF16)

    g64 = _block_diag_mean(384, HD)
    g32 = _block_diag_mean(384, B_DK)
    rope_a = _rope_tables(s_l, HD)
    rope_b = _rope_tables(s_l, B_DK)

    ck = cache_a_k.reshape(db, depth, past, 128)
    cv = cache_a_v.reshape(db, depth, past, 128)
    caches = (jnp.concatenate([ck, swap_kv(ck)], axis=-1).astype(BF16),
              jnp.concatenate([cv, swap_kv(cv)], axis=-1).astype(BF16),
              cache_b_k.reshape(db, depth, past, 384).astype(BF16),
              cache_b_v.reshape(db, depth, past, 384).astype(BF16))

    xp = x_prompt.reshape(nb * s_p, d)
    xs = x_sample.reshape(db * s_l, d)
    tps_p = s_p // tm_p
    tps_l = s_l // tm_l
    row_p = lambda i: 0
    row_l = lambda i: 1 + i // tps_l
    row_l_peer = lambda i: 1 + i // (s_l // tm_peer)

    new = [[], [], [], []]
    for l in range(depth):
        lam_init = 0.8 - 0.6 * math.exp(-0.3 * l)
        gains = ((jnp.tile(qn_a[l], A_HEADS) * HD ** -0.5)[None, :],
                 jnp.tile(kn_a[l], 2 * A_KV)[None, :],
                 (jnp.tile(qn_b[l], 2 * B_HEADS) * B_DK ** -0.5)[None, :],
                 jnp.tile(kn_b[l], 2 * B_HEADS)[None, :])
        lams = (lam_q1[l][None, :], lam_k1[l][None, :], lam_q2[l][None, :], lam_k2[l][None, :])
        gb = (jnp.tile(subln_g[l], B_HEADS) * (1.0 - lam_init))[None, :]
        post_args = (out_norm_a[l][None, :], gb, g64, out_norm_c[l][None, :], conv_w[l], conv_b[l][None, :],
                     wout)
        n1g = norm1_g[l][None, :]
        n2g = norm2_g[l][None, :]

        qa, ka2, va2, qb, kb, vb, u, cbg, cka, cva, ckb, cvb = _pre_mix(
            xp, mod[l], row_p, n1g, win, gains, (g64, g32), None, layer=l,
            tm=tm_pre_p, tiles_per_seq=1, emit_cache=True)
        new[0].append(cka.reshape(nb, s_p, A_KV, HD))
        new[1].append(cva.reshape(nb, s_p, A_KV, HD))
        new[2].append(ckb.reshape(nb, s_p, B_HEADS, 2, B_DK))
        new[3].append(cvb.reshape(nb, s_p, B_HEADS, 2 * B_DK))
        r3 = lambda a: a.reshape(nb, s_p, a.shape[-1])
        oa, ob = _attention(r3(qa), r3(ka2), r3(va2), r3(qb), r3(kb), r3(vb), None, lams,
                            layer=l, lam_init=lam_init, tq=tq_p)
        xp = _post_mix(oa.reshape(nb * s_p, 384), ob.reshape(nb * s_p, 384), u, cbg, xp, mod[l], row_p,
                       *post_args, layer=l, tm=tm_p, tiles_per_seq=tps_p)
        xp = _peer_split(xp, mod[l], row_p, n2g, wqt, keys, u_tab, vt_tab, layer=l, tm=tm_peer, ec=ec)

        qa, ka2, va2, qb, kb, vb, u, cbg = _pre_mix(
            xs, mod[l], row_l, n1g, win, gains, (g64, g32), rope_a + rope_b, layer=l,
            tm=tm_l, tiles_per_seq=tps_l, emit_cache=False)
        r3 = lambda a: a.reshape(db, s_l, a.shape[-1])
        oa, ob = _attention(r3(qa), r3(ka2), r3(va2), r3(qb), r3(kb), r3(vb), caches, lams,
                            layer=l, lam_init=lam_init, tq=tq_l)
        xs = _post_mix(oa.reshape(db * s_l, 384), ob.reshape(db * s_l, 384), u, cbg, xs, mod[l], row_l,
                       *post_args, layer=l, tm=tm_l, tiles_per_seq=tps_l)
        xs = _peer_split(xs, mod[l], row_l_peer, n2g, wqt, keys, u_tab, vt_tab, layer=l, tm=tm_peer, ec=ec)

    return (xp.reshape(nb, s_p, d), xs.reshape(db, s_l, d),
            jnp.stack(new[0], axis=1), jnp.stack(new[1], axis=1),
            jnp.stack(new[2], axis=1), jnp.stack(new[3], axis=1))
```
